```python
import math
import jax, jax.numpy as jnp
from jax import lax
import numpy as np

D_MODEL = 1024
BATCH = 16
SEQ = 4096
DEPTH = 1
DEC_BATCH = 8
DEC_SEQ = 32
PAST_LEN = 1024

CHUNK = 64
Q_BLOCK = 128
ROPE_THETA = 10000.0
RMS_EPS = 1e-6
SUBLN_EPS = 1e-5
NEG_INF = -1e30

MLA_H = 8
Q_LORA = 256
KV_LORA = 128
MLA_NOPE = 64
MLA_ROPE = 32
MLA_QK = MLA_NOPE + MLA_ROPE
MLA_V = 64
MLA_SCALE = MLA_QK ** -0.5

DIFF_H = 8
DIFF_D = 32
DIFF_QK = DIFF_H * 2 * DIFF_D
DIFF_VW = DIFF_H * 2 * DIFF_D
DIFF_SCALE = DIFF_D ** -0.5

IN_SPLITS = (Q_LORA, KV_LORA, MLA_ROPE, DIFF_QK, DIFF_QK, DIFF_VW, D_MODEL, D_MODEL)
IN_COLS = sum(IN_SPLITS)

N_EXPERTS = 32
TOP_K = 4
D_FF = D_MODEL
SWIGLU_ALPHA = 1.702
SWIGLU_LIMIT = 7.0
MOE_BLOCK = 256

kernel_name = 'hybrid_mla_diffattn_moe_streaming_step'


def rmsnorm(x, g, eps=RMS_EPS):
    xf = x.astype(jnp.float32)
    y = xf * lax.rsqrt(jnp.mean(xf * xf, axis=-1, keepdims=True) + eps)
    return (y * g.astype(jnp.float32)).astype(x.dtype)


def rope(x, pos):
    half = x.shape[-1] // 2
    inv = ROPE_THETA ** (-jnp.arange(half, dtype=jnp.float32) / half)
    ang = pos.astype(jnp.float32)[:, None] * inv[None, :]
    shape = (1, pos.shape[0]) + (1,) * (x.ndim - 3) + (half,)
    cos = jnp.cos(ang).reshape(shape)
    sin = jnp.sin(ang).reshape(shape)
    x1 = x[..., :half].astype(jnp.float32)
    x2 = x[..., half:].astype(jnp.float32)
    return jnp.concatenate([x1 * cos - x2 * sin, x2 * cos + x1 * sin], axis=-1).astype(x.dtype)


def chunk_mask(q_pos, k_pos):
    return ((k_pos[None, :] // CHUNK) <= (q_pos[:, None] // CHUNK))[None, None]


def masked_softmax(s, mask):
    return jax.nn.softmax(jnp.where(mask, s.astype(jnp.float32), NEG_INF), axis=-1)


def query_blocks(fn, qs, pos):
    l = pos.shape[0]
    if l <= Q_BLOCK:
        return fn(*qs, pos)
    nb = l // Q_BLOCK
    def split(a):
        return jnp.moveaxis(a.reshape((a.shape[0], nb, Q_BLOCK) + a.shape[2:]), 1, 0)
    outs = lax.map(lambda args: fn(*args), tuple(split(a) for a in qs) + (pos.reshape(nb, Q_BLOCK),))
    return tuple(jnp.moveaxis(o, 0, 1).reshape((o.shape[1], l) + o.shape[3:]) for o in outs)


def attention_block(xn, pos, past, lw, lam_init):
    b, l, _ = xn.shape
    split_idx = np.cumsum(IN_SPLITS)[:-1].tolist()
    proj = xn @ lw['w_in']
    q_down, kv_down, k_rope, dq, dk, dv, g_mla, g_diff = jnp.split(proj, split_idx, axis=-1)

    cq = rmsnorm(q_down, lw['g_q_a'])
    q = (cq @ lw['w_uq']).reshape(b, l, MLA_H, MLA_QK)
    q_nope = q[..., :MLA_NOPE]
    q_pe = rope(q[..., MLA_NOPE:], pos)
    q_lat = jnp.einsum('blhn,chn->blhc', q_nope, lw['w_uk'])
    ckv = rmsnorm(kv_down, lw['g_kv_a'])
    kpe = rope(k_rope, pos)

    dq = rope(dq.reshape(b, l, DIFF_H, 2, DIFF_D), pos)
    dk = rope(dk.reshape(b, l, DIFF_H, 2, DIFF_D), pos)
    dv = dv.reshape(b, l, DIFF_H, 2 * DIFF_D)

    if past is None:
        ckv_all, kpe_all, dk_all, dv_all = ckv, kpe, dk, dv
        k_pos = pos
    else:
        p_ckv, p_kpe, p_dk, p_dv = past
        ckv_all = jnp.concatenate([p_ckv, ckv], axis=1)
        kpe_all = jnp.concatenate([p_kpe, kpe], axis=1)
        dk_all = jnp.concatenate([p_dk, dk], axis=1)
        dv_all = jnp.concatenate([p_dv, dv], axis=1)
        k_pos = jnp.arange(p_ckv.shape[1] + l, dtype=jnp.int32)
    k1_all = dk_all[..., 0, :]
    k2_all = dk_all[..., 1, :]

    f32 = jnp.float32
    lam = (jnp.exp(jnp.sum(lw['lambda_q1'].astype(f32) * lw['lambda_k1'].astype(f32)))
           - jnp.exp(jnp.sum(lw['lambda_q2'].astype(f32) * lw['lambda_k2'].astype(f32)))
           + lam_init)

    def core(ql, qp, dqb, qpos):
        mask = chunk_mask(qpos, k_pos)
        s = (jnp.einsum('bqhc,bkc->bhqk', ql, ckv_all)
             + jnp.einsum('bqhr,bkr->bhqk', qp, kpe_all)) * MLA_SCALE
        p = masked_softmax(s, mask).astype(ckv_all.dtype)
        o_lat = jnp.einsum('bhqk,bkc->bqhc', p, ckv_all)
        s1 = jnp.einsum('bqhd,bkhd->bhqk', dqb[..., 0, :], k1_all) * DIFF_SCALE
        s2 = jnp.einsum('bqhd,bkhd->bhqk', dqb[..., 1, :], k2_all) * DIFF_SCALE
        a = (masked_softmax(s1, mask) - lam * masked_softmax(s2, mask)).astype(dv_all.dtype)
        o_diff = jnp.einsum('bhqk,bkhe->bqhe', a, dv_all)
        return o_lat, o_diff

    o_lat, o_diff = query_blocks(core, (q_lat, q_pe, dq), pos)
    y_mla = jnp.einsum('blhc,chv->blhv', o_lat, lw['w_uv']).reshape(b, l, MLA_H * MLA_V) @ lw['w_o_mla']
    o_diff = rmsnorm(o_diff, lw['g_subln'], SUBLN_EPS) * (1.0 - lam_init)
    y_diff = o_diff.reshape(b, l, DIFF_VW) @ lw['w_o_diff']
    merged = jax.nn.sigmoid(g_mla) * y_mla + jax.nn.sigmoid(g_diff) * y_diff
    return merged @ lw['w_out'], (ckv, kpe, dk, dv)


def clamped_swiglu(h):
    gate = jnp.minimum(h[..., :D_FF], SWIGLU_LIMIT)
    up = jnp.clip(h[..., D_FF:], -SWIGLU_LIMIT, SWIGLU_LIMIT)
    return gate * jax.nn.sigmoid(SWIGLU_ALPHA * gate) * (up + 1.0)


def moe_ffn(x2d, router_w, router_b, w_gate_up, b_gate_up, w_down, b_down):
    t, d = x2d.shape
    logits = jnp.dot(x2d, router_w).astype(jnp.float32) + router_b.astype(jnp.float32)
    top_v, top_i = lax.top_k(logits, TOP_K)
    gate = jax.nn.softmax(top_v, axis=-1)
    flat_e = top_i.reshape(-1).astype(jnp.int32)
    n_assign = t * TOP_K
    order = jnp.argsort(flat_e)
    sorted_e = flat_e[order]
    counts = jnp.bincount(flat_e, length=N_EXPERTS)
    padded = (counts + MOE_BLOCK - 1) // MOE_BLOCK * MOE_BLOCK
    pend = jnp.cumsum(padded)
    pstart = pend - padded
    start = jnp.cumsum(counts) - counts
    dest = pstart[sorted_e] + jnp.arange(n_assign, dtype=jnp.int32) - start[sorted_e]
    n_blocks = -(-n_assign // MOE_BLOCK) + N_EXPERTS
    n_slots = n_blocks * MOE_BLOCK
    slot_tok = jnp.full((n_slots,), t, jnp.int32).at[dest].set((order // TOP_K).astype(jnp.int32))
    slot_gate = jnp.zeros((n_slots,), jnp.float32).at[dest].set(gate.reshape(-1)[order])
    block_e = jnp.minimum(jnp.searchsorted(pend, jnp.arange(n_blocks, dtype=jnp.int32) * MOE_BLOCK, side='right'),
                          N_EXPERTS - 1)
    x_pad = jnp.concatenate([x2d, jnp.zeros((1, d), x2d.dtype)], axis=0)

    def step(acc, blk):
        tok, g, e = blk
        h = x_pad[tok] @ w_gate_up[e] + b_gate_up[e]
        y = clamped_swiglu(h) @ w_down[e] + b_down[e]
        return acc.at[tok].add(y.astype(jnp.float32) * g[:, None]), None

    acc, _ = lax.scan(step, jnp.zeros((t + 1, d), jnp.float32),
                      (slot_tok.reshape(n_blocks, MOE_BLOCK), slot_gate.reshape(n_blocks, MOE_BLOCK), block_e))
    return acc[:t].astype(x2d.dtype)


def setup_inputs(seed: int = 0) -> dict:
    key = jax.random.key(seed)
    ks = jax.random.split(key, 32)
    f32 = jnp.float32
    def nrm(k, shape, scale):
        return jax.random.normal(k, shape, f32) * scale
    def gain(k, shape):
        return 1.0 + 0.02 * jax.random.normal(k, shape, f32)
    L = DEPTH
    return {
        'x_prompt': nrm(ks[0], (BATCH, SEQ, D_MODEL), 1.0),
        'x_sample': nrm(ks[1], (DEC_BATCH, DEC_SEQ, D_MODEL), 1.0),
        'cache_mla_ckv': nrm(ks[2], (L, DEC_BATCH, PAST_LEN, KV_LORA), 1.0),
        'cache_mla_kpe': nrm(ks[3], (L, DEC_BATCH, PAST_LEN, MLA_ROPE), 1.0),
        'cache_diff_k': nrm(ks[4], (L, DEC_BATCH, PAST_LEN, DIFF_H, 2, DIFF_D), 1.0),
        'cache_diff_v': nrm(ks[5], (L, DEC_BATCH, PAST_LEN, DIFF_H, 2 * DIFF_D), 1.0),
        'g_attn_norm': gain(ks[6], (L, D_MODEL)),
        'w_in': nrm(ks[7], (L, D_MODEL, IN_COLS), D_MODEL ** -0.5),
        'g_q_a': gain(ks[8], (L, Q_LORA)),
        'w_uq': nrm(ks[9], (L, Q_LORA, MLA_H * MLA_QK), Q_LORA ** -0.5),
        'g_kv_a': gain(ks[10], (L, KV_LORA)),
        'w_uk': nrm(ks[11], (L, KV_LORA, MLA_H, MLA_NOPE), KV_LORA ** -0.5),
        'w_uv': nrm(ks[12], (L, KV_LORA, MLA_H, MLA_V), KV_LORA ** -0.5),
        'w_o_mla': nrm(ks[13], (L, MLA_H * MLA_V, D_MODEL), (MLA_H * MLA_V) ** -0.5),
        'lambda_q1': nrm(ks[14], (L, DIFF_D), 0.1),
        'lambda_k1': nrm(ks[15], (L, DIFF_D), 0.1),
        'lambda_q2': nrm(ks[16], (L, DIFF_D), 0.1),
        'lambda_k2': nrm(ks[17], (L, DIFF_D), 0.1),
        'g_subln': gain(ks[18], (L, 2 * DIFF_D)),
        'w_o_diff': nrm(ks[19], (L, DIFF_VW, D_MODEL), DIFF_VW ** -0.5),
        'w_out': nrm(ks[20], (L, D_MODEL, D_MODEL), D_MODEL ** -0.5),
        'g_ffn_norm': gain(ks[21], (L, D_MODEL)),
        'router_w': nrm(ks[22], (L, D_MODEL, N_EXPERTS), D_MODEL ** -0.5),
        'router_b': nrm(ks[23], (L, N_EXPERTS), 0.01),
        'w_gate_up': nrm(ks[24], (L, N_EXPERTS, D_MODEL, 2 * D_FF), D_MODEL ** -0.5),
        'b_gate_up': nrm(ks[25], (L, N_EXPERTS, 2 * D_FF), 0.01),
        'w_down': nrm(ks[26], (L, N_EXPERTS, D_FF, D_MODEL), D_FF ** -0.5),
        'b_down': nrm(ks[27], (L, N_EXPERTS, D_MODEL), 0.01),
        'g_final': gain(ks[28], (D_MODEL,)),
    }


def reference(x_prompt, x_sample, cache_mla_ckv, cache_mla_kpe, cache_diff_k, cache_diff_v,
              g_attn_norm, w_in, g_q_a, w_uq, g_kv_a, w_uk, w_uv, w_o_mla,
              lambda_q1, lambda_k1, lambda_q2, lambda_k2, g_subln, w_o_diff, w_out,
              g_ffn_norm, router_w, router_b, w_gate_up, b_gate_up, w_down, b_down, g_final):
    pos_p = jnp.arange(x_prompt.shape[1], dtype=jnp.int32)
    past_len = cache_mla_ckv.shape[2]
    pos_s = jnp.arange(past_len, past_len + x_sample.shape[1], dtype=jnp.int32)
    hp, hs = x_prompt, x_sample
    new_p = ([], [], [], [])
    new_s = ([], [], [], [])
    for i in range(DEPTH):
        lam_init = 0.8 - 0.6 * math.exp(-0.3 * i)
        lw = {'w_in': w_in[i], 'g_q_a': g_q_a[i], 'w_uq': w_uq[i], 'g_kv_a': g_kv_a[i],
              'w_uk': w_uk[i], 'w_uv': w_uv[i], 'w_o_mla': w_o_mla[i],
              'lambda_q1': lambda_q1[i], 'lambda_k1': lambda_k1[i],
              'lambda_q2': lambda_q2[i], 'lambda_k2': lambda_k2[i],
              'g_subln': g_subln[i], 'w_o_diff': w_o_diff[i], 'w_out': w_out[i]}
        a_p, st_p = attention_block(rmsnorm(hp, g_attn_norm[i]), pos_p, None, lw, lam_init)
        a_s, st_s = attention_block(rmsnorm(hs, g_attn_norm[i]), pos_s,
                                    (cache_mla_ckv[i], cache_mla_kpe[i], cache_diff_k[i], cache_diff_v[i]),
                                    lw, lam_init)
        for lst, arr in zip(new_p, st_p):
            lst.append(arr)
        for lst, arr in zip(new_s, st_s):
            lst.append(arr)
        hp = hp + a_p
        hs = hs + a_s
        n_p = hp.shape[0] * hp.shape[1]
        tokens = jnp.concatenate([rmsnorm(hp, g_ffn_norm[i]).reshape(n_p, D_MODEL),
                                  rmsnorm(hs, g_ffn_norm[i]).reshape(-1, D_MODEL)], axis=0)
        f = moe_ffn(tokens, router_w[i], router_b[i], w_gate_up[i], b_gate_up[i], w_down[i], b_down[i])
        hp = hp + f[:n_p].reshape(hp.shape)
        hs = hs + f[n_p:].reshape(hs.shape)
    y_prompt = rmsnorm(hp, g_final)
    y_sample = rmsnorm(hs, g_final)
    ckv_p, kpe_p, dk_p, dv_p = [jnp.stack(v) for v in new_p]
    ckv_s, kpe_s, dk_s, dv_s = [jnp.stack(v) for v in new_s]
    return (y_prompt, y_sample, ckv_p, kpe_p, dk_p, dv_p, ckv_s, kpe_s, dk_s, dv_s)
```

```python
import functools
import math

import jax
import jax.numpy as jnp
from jax import lax
from jax.experimental import pallas as pl
from jax.experimental.pallas import tpu as pltpu

F32 = jnp.float32
BF16 = jnp.bfloat16

D_MODEL = 1024
CHUNK = 64
CHUNK_SHIFT = 6
ROPE_THETA = 10000.0
RMS_EPS = 1e-6
SUBLN_EPS = 1e-5
NEG_INF = -1e30

MLA_H = 8
Q_LORA = 256
KV_LORA = 128
MLA_NOPE = 64
MLA_ROPE = 32
MLA_QK = MLA_NOPE + MLA_ROPE
MLA_V = 64
MLA_SCALE = MLA_QK ** -0.5

DIFF_H = 8
DIFF_D = 32
DIFF_W = DIFF_H * 2 * DIFF_D
DIFF_SCALE = DIFF_D ** -0.5
N_PAIR = DIFF_W // 128

N_EXPERTS = 32
TOP_K = 4
D_FF = D_MODEL
SWIGLU_ALPHA = 1.702
SWIGLU_LIMIT = 7.0

LANES = 128
QK_PAD = 256
ONE_COL = KV_LORA + MLA_ROPE
VMEM_LIMIT = 56 * 1024 * 1024

A_QD = 0
A_KV = A_QD + Q_LORA
A_DQ = A_KV + KV_LORA
A_DK = A_DQ + DIFF_W
A_DV = A_DK + DIFF_W
A_KR = A_DV + DIFF_W
A_COLS = A_KR + LANES


def _rms(x, g, eps):
    return x * lax.rsqrt(jnp.mean(x * x, axis=-1, keepdims=True) + eps) * g


def _dot(a, b):
    return jnp.dot(a, b, preferred_element_type=F32)


def _dot_nt(a, b):
    return lax.dot_general(a, b, (((1,), (1,)), ((), ())), preferred_element_type=F32)


def _proj_kernel(x_ref, cos_ref, sin_ref, gattn_ref, wa_ref, gqa_ref, wuq_ref, wn_ref, we_ref, gkva_ref,
                 ckv_ref, kpe_ref, dk_ref, dv_ref, qm_ref, kvm_ref, dqb_ref, dkb_ref, dvb_ref):
    tm = x_ref.shape[1]
    x = x_ref[0]
    xn = _rms(x, gattn_ref[...], RMS_EPS).astype(BF16)
    p = _dot(xn, wa_ref[...])

    cos = cos_ref[...]
    sin = sin_ref[...]
    lane = lax.broadcasted_iota(jnp.int32, (tm, LANES), 1)
    first_half = (lane & (DIFF_D - 1)) < (DIFF_D // 2)

    def rope128(xc):
        fwd = pltpu.roll(xc, LANES - DIFF_D // 2, 1)
        bwd = pltpu.roll(xc, DIFF_D // 2, 1)
        return xc * cos + jnp.where(first_half, fwd, bwd) * sin

    def rope_wide(xw):
        return jnp.concatenate([rope128(xw[:, c * LANES:(c + 1) * LANES]) for c in range(xw.shape[1] // LANES)], axis=1)

    cq = _rms(p[:, A_QD:A_QD + Q_LORA], gqa_ref[...], RMS_EPS).astype(BF16)
    q = _dot(cq, wuq_ref[...])
    qn = q[:, :MLA_H * MLA_NOPE].astype(BF16)
    qr = rope_wide(q[:, MLA_H * MLA_NOPE:]).astype(BF16)
    for h in range(MLA_H):
        qe = _dot(qn, wn_ref[h]) + _dot(qr, we_ref[h])
        qm_ref[0, h] = (qe * MLA_SCALE).astype(BF16)

    ckv = _rms(p[:, A_KV:A_KV + KV_LORA], gkva_ref[...], RMS_EPS)
    ckv_ref[0] = ckv
    kseg = rope128(p[:, A_KR:A_KR + LANES])
    kpe_ref[0] = kseg[:, :MLA_ROPE]
    one = jnp.where(lane == MLA_ROPE, 1.0, 0.0)
    kvm_ref[0] = jnp.concatenate([ckv, kseg + one], axis=1).astype(BF16)

    dq = rope_wide(p[:, A_DQ:A_DQ + DIFF_W])
    dqb_ref[0] = (dq * DIFF_SCALE).astype(BF16)
    dk = rope_wide(p[:, A_DK:A_DK + DIFF_W])
    dk_ref[0] = dk
    dkb_ref[0] = dk.astype(BF16)
    dv = p[:, A_DV:A_DV + DIFF_W]
    dv_ref[0] = dv
    dvb_ref[0] = dv.astype(BF16)


def _proj(x, cos, sin, w):
    b, l, d = x.shape
    tm = min(512, l)
    assert l % tm == 0
    grid = (l // tm, b)

    def full(a):
        return pl.BlockSpec(a.shape, lambda i, j, n=a.ndim: (0,) * n)

    def tok(width):
        return pl.BlockSpec((1, tm, width), lambda i, j: (j, i, 0))

    weights = (w['g_attn'], w['w_a'], w['g_q_a'], w['w_uq'], w['w_n'], w['w_e'], w['g_kv_a'])
    out_shape = (
        jax.ShapeDtypeStruct((b, l, KV_LORA), F32),
        jax.ShapeDtypeStruct((b, l, MLA_ROPE), F32),
        jax.ShapeDtypeStruct((b, l, DIFF_W), F32),
        jax.ShapeDtypeStruct((b, l, DIFF_W), F32),
        jax.ShapeDtypeStruct((b, MLA_H, l, QK_PAD), BF16),
        jax.ShapeDtypeStruct((b, l, QK_PAD), BF16),
        jax.ShapeDtypeStruct((b, l, DIFF_W), BF16),
        jax.ShapeDtypeStruct((b, l, DIFF_W), BF16),
        jax.ShapeDtypeStruct((b, l, DIFF_W), BF16),
    )
    out_specs = (tok(KV_LORA), tok(MLA_ROPE), tok(DIFF_W), tok(DIFF_W),
                 pl.BlockSpec((1, MLA_H, tm, QK_PAD), lambda i, j: (j, 0, i, 0)),
                 tok(QK_PAD), tok(DIFF_W), tok(DIFF_W), tok(DIFF_W))
    tab = pl.BlockSpec((tm, LANES), lambda i, j: (i, 0))
    return pl.pallas_call(
        _proj_kernel,
        grid=grid,
        in_specs=[tok(d), tab, tab] + [full(a) for a in weights],
        out_specs=out_specs,
        out_shape=out_shape,
        compiler_params=pltpu.CompilerParams(dimension_semantics=("arbitrary", "arbitrary"),
                                             vmem_limit_bytes=VMEM_LIMIT),
        name="proj",
    )(x, cos, sin, *weights)


def _attn_kernel(qm_ref, dq_ref, kvm_ref, dk_ref, dv_ref, lam_ref, olat_ref, odiff_ref,
                 vext_ref, m_ref, acc_ref, *, tq, tk, q_pos0, n_keys, lam_init):
    i = pl.program_id(1)
    lk = kvm_ref.shape[1]

    @pl.when(i == 0)
    def _():
        ones_col = jnp.where(lax.broadcasted_iota(jnp.int32, (lk, LANES), 1) == 0, 1.0, 0.0).astype(BF16)
        for p in range(N_PAIR):
            vext_ref[p, :, 0:LANES] = dv_ref[0, :, p * LANES:(p + 1) * LANES]
            vext_ref[p, :, LANES:2 * LANES] = ones_col

    q_lo = q_pos0 + i * tq
    q_hi = q_lo + tq - 1
    vis_all = jnp.minimum((q_lo // CHUNK + 1) * CHUNK, n_keys)
    vis_any = jnp.minimum((q_hi // CHUNK + 1) * CHUNK, n_keys)
    n_full = vis_all // tk
    n_tot = (vis_any + tk - 1) // tk

    def flash(q, kfn, vfn, lcol):
        r = q.shape[0]
        m_ref[0:r] = jnp.full((r, LANES), NEG_INF, F32)
        acc_ref[0:r] = jnp.zeros((r, 2 * LANES), F32)
        q_row = lax.broadcasted_iota(jnp.int32, (r, 1), 0) & (tq - 1)
        q_chunk = lax.shift_right_logical(q_lo + q_row, CHUNK_SHIFT)

        def step(kt, masked):
            start = pl.multiple_of(kt * tk, tk)
            s = _dot_nt(q, kfn(start))
            if masked:
                k_pos = kt * tk + lax.broadcasted_iota(jnp.int32, (1, tk), 1)
                k_chunk = jnp.where(k_pos < n_keys, lax.shift_right_logical(k_pos, CHUNK_SHIFT), jnp.int32(2 ** 30))
                s = jnp.where(k_chunk <= q_chunk, s, NEG_INF)
            m_old = m_ref[0:r, 0:1]
            m_new = jnp.maximum(m_old, jnp.max(s, axis=1, keepdims=True))
            alpha = jnp.exp(m_old - m_new)
            p = jnp.exp(s - m_new).astype(BF16)
            acc_ref[0:r] = alpha * acc_ref[0:r] + _dot(p, vfn(start))
            m_ref[0:r] = jnp.broadcast_to(m_new, (r, LANES))

        def full_step(kt, c):
            step(kt, False)
            return c

        def masked_step(kt, c):
            step(kt, True)
            return c

        lax.fori_loop(0, n_full, full_step, 0)
        lax.fori_loop(n_full, n_tot, masked_step, 0)
        acc = acc_ref[0:r]
        return acc[:, 0:LANES] / acc[:, lcol:lcol + 1]

    qm = qm_ref[0].reshape(MLA_H * tq, QK_PAD)
    kv = lambda start: kvm_ref[0, pl.ds(start, tk), :]
    o = flash(qm, kv, kv, ONE_COL)
    for h in range(MLA_H):
        olat_ref[0, :, h * LANES:(h + 1) * LANES] = o[h * tq:(h + 1) * tq].astype(BF16)

    lam = (jnp.exp(jnp.sum(lam_ref[0:1] * lam_ref[1:2], axis=1, keepdims=True))
           - jnp.exp(jnp.sum(lam_ref[2:3] * lam_ref[3:4], axis=1, keepdims=True)) + lam_init)

    lane = lax.broadcasted_iota(jnp.int32, (tq, LANES), 1)
    for p in range(N_PAIR):
        qp = dq_ref[0, :, p * LANES:(p + 1) * LANES].astype(F32)
        seg = lax.shift_right_logical(lane, 5)
        rows = jnp.concatenate([jnp.where(seg == j, qp, 0.0) for j in range(4)], axis=0).astype(BF16)
        kf = lambda start, p=p: dk_ref[0, pl.ds(start, tk), p * LANES:(p + 1) * LANES]
        vf = lambda start, p=p: vext_ref[p, pl.ds(start, tk), :]
        n = flash(rows, kf, vf, LANES)
        a0 = n[0:tq] - lam * n[tq:2 * tq]
        a1 = n[2 * tq:3 * tq] - lam * n[3 * tq:4 * tq]
        odiff_ref[0, :, p * LANES:(p + 1) * LANES] = jnp.where(lane < 2 * DIFF_D, a0, a1)


def _attention(qm, dqb, kvm, dkb, dvb, lam_rows, *, q_pos0, n_keys, lam_init):
    b, _, l, _ = qm.shape
    lk = kvm.shape[1]
    tq = min(256, l)
    tk = 256
    assert l % tq == 0 and lk % tk == 0
    kern = functools.partial(_attn_kernel, tq=tq, tk=tk, q_pos0=q_pos0, n_keys=n_keys, lam_init=lam_init)
    per_batch = lambda width: pl.BlockSpec((1, lk, width), lambda j, i: (j, 0, 0))
    return pl.pallas_call(
        kern,
        grid=(b, l // tq),
        in_specs=[pl.BlockSpec((1, MLA_H, tq, QK_PAD), lambda j, i: (j, 0, i, 0)),
                  pl.BlockSpec((1, tq, DIFF_W), lambda j, i: (j, i, 0)),
                  per_batch(QK_PAD), per_batch(DIFF_W), per_batch(DIFF_W),
                  pl.BlockSpec(lam_rows.shape, lambda j, i: (0, 0))],
        out_specs=(pl.BlockSpec((1, tq, MLA_H * KV_LORA), lambda j, i: (j, i, 0)),
                   pl.BlockSpec((1, tq, DIFF_W), lambda j, i: (j, i, 0))),
        out_shape=(jax.ShapeDtypeStruct((b, l, MLA_H * KV_LORA), BF16),
                   jax.ShapeDtypeStruct((b, l, DIFF_W), F32)),
        scratch_shapes=[pltpu.VMEM((N_PAIR, lk, 2 * LANES), BF16),
                        pltpu.VMEM((MLA_H * tq, LANES), F32),
                        pltpu.VMEM((MLA_H * tq, 2 * LANES), F32)],
        compiler_params=pltpu.CompilerParams(dimension_semantics=("arbitrary", "arbitrary"),
                                             vmem_limit_bytes=VMEM_LIMIT),
        name="attn",
    )(qm, dqb, kvm, dkb, dvb, lam_rows)


def _post_kernel(x_ref, olat_ref, odiff_ref, cnt_in_ref, su_ref, gattn_ref, wg_ref, wuv_ref, womla_ref, bd_ref,
                 gsub_ref, wodiff_ref, wout_ref, gffn_ref, rwt_ref, rb_ref,
                 h_ref, tok_ref, topi_ref, rank_ref, gate_ref, cnt_ref, *, lam_init):
    tm = x_ref.shape[0]
    x = x_ref[...]
    xn = _rms(x, gattn_ref[...], RMS_EPS).astype(BF16)
    g = _dot(xn, wg_ref[...])
    y_mla = _dot(_dot(olat_ref[...], wuv_ref[...]).astype(BF16), womla_ref[...])

    od = odiff_ref[...]
    sq = od * od
    sq_hi = sq.astype(BF16)
    sq_lo = (sq - sq_hi.astype(F32)).astype(BF16)
    ms = _dot(sq_hi, bd_ref[...]) + _dot(sq_lo, bd_ref[...])
    on = od * lax.rsqrt(ms + SUBLN_EPS) * gsub_ref[...] * (1.0 - lam_init)
    y_diff = _dot(on.astype(BF16), wodiff_ref[...])

    merged = jax.nn.sigmoid(g[:, :D_MODEL]) * y_mla + jax.nn.sigmoid(g[:, D_MODEL:]) * y_diff
    h = x + _dot(merged.astype(BF16), wout_ref[...])
    h_ref[...] = h
    tok = _rms(h, gffn_ref[...], RMS_EPS)
    tok_ref[...] = tok

    logits = _dot_nt(rwt_ref[...], tok.astype(BF16)) + rb_ref[:, 0:1]
    e_iota = lax.broadcasted_iota(jnp.int32, (N_EXPERTS, tm), 0).astype(F32)
    work = logits
    vals, hots = [], []
    for k in range(TOP_K):
        mx = jnp.max(work, axis=0, keepdims=True)
        idx = jnp.min(jnp.where(work == mx, e_iota, float(N_EXPERTS)), axis=0, keepdims=True)
        hot = e_iota == idx
        work = jnp.where(hot, -jnp.inf, work)
        vals.append(mx)
        hots.append(hot)
        topi_ref[k:k + 1, :] = idx.astype(jnp.int32)
    exps = [jnp.exp(v - vals[0]) for v in vals]
    den = exps[0] + exps[1] + exps[2] + exps[3]
    for k in range(TOP_K):
        gate_ref[k:k + 1, :] = exps[k] / den

    @pl.when(pl.program_id(0) == 0)
    def _():
        cnt_ref[...] = cnt_in_ref[...]

    sel = sum(jnp.where(hot, 1.0, 0.0) for hot in hots)
    before = _dot(sel.astype(BF16), su_ref[...]) + cnt_ref[:, 0:1]
    for k in range(TOP_K):
        rank_ref[k:k + 1, :] = jnp.sum(jnp.where(hots[k], before, 0.0), axis=0, keepdims=True).astype(jnp.int32)
    cnt_ref[...] = cnt_ref[...] + jnp.sum(sel, axis=1, keepdims=True)


def _post(x2d, olat2d, odiff2d, cnt_in, w, *, lam_init):
    t, d = x2d.shape
    tm = min(512, t)
    assert t % tm == 0
    su = (lax.broadcasted_iota(jnp.int32, (tm, tm), 0) < lax.broadcasted_iota(jnp.int32, (tm, tm), 1)).astype(BF16)
    weights = (w['g_attn'], w['w_g'], w['w_uv_bd'], w['w_o_mla'], w['bd64'], w['g_subln'], w['w_o_diff'],
               w['w_out'], w['g_ffn'], w['rw_t'], w['rb'])

    def full(a):
        return pl.BlockSpec(a.shape, lambda i, n=a.ndim: (0,) * n)

    tokspec = lambda width: pl.BlockSpec((tm, width), lambda i: (i, 0))
    small = pl.BlockSpec((TOP_K, tm), lambda i: (0, i))
    kern = functools.partial(_post_kernel, lam_init=lam_init)
    return pl.pallas_call(
        kern,
        grid=(t // tm,),
        in_specs=[tokspec(d), tokspec(MLA_H * KV_LORA), tokspec(DIFF_W), full(cnt_in), full(su)]
                 + [full(a) for a in weights],
        out_specs=(tokspec(d), tokspec(d), small, small, small, full(cnt_in)),
        out_shape=(jax.ShapeDtypeStruct((t, d), F32), jax.ShapeDtypeStruct((t, d), F32),
                   jax.ShapeDtypeStruct((TOP_K, t), jnp.int32), jax.ShapeDtypeStruct((TOP_K, t), jnp.int32),
                   jax.ShapeDtypeStruct((TOP_K, t), F32), jax.ShapeDtypeStruct(cnt_in.shape, F32)),
        compiler_params=pltpu.CompilerParams(dimension_semantics=("arbitrary",), vmem_limit_bytes=VMEM_LIMIT),
        name="post",
    )(x2d, olat2d, odiff2d, cnt_in, su, *weights)


def _dispatch_kernel(pos_ref, tok_ref, *rest):
    xs_ref, sem = rest[-2], rest[-1]
    tm = tok_ref.shape[0]

    def row_copy(t, k):
        return pltpu.make_async_copy(tok_ref.at[pl.ds(t, 1)], xs_ref.at[pl.ds(pos_ref[k, t], 1)], sem)

    def issue(t, c):
        for k in range(TOP_K):
            row_copy(t, k).start()
        return c

    lax.fori_loop(0, tm, issue, 0)

    def drain(t, c):
        for k in range(TOP_K):
            row_copy(t, k).wait()
        return c

    lax.fori_loop(0, tm, drain, 0)


def _dispatch(pos, tok, xs, n_slots):
    t, d = tok.shape
    tm = min(256, t)
    assert t % tm == 0
    in_specs = [pl.BlockSpec((TOP_K, tm), lambda i: (0, i), memory_space=pltpu.SMEM),
                pl.BlockSpec((tm, d), lambda i: (i, 0))]
    args = [pos, tok]
    aliases = {}
    if xs is not None:
        in_specs.append(pl.BlockSpec(memory_space=pl.ANY))
        args.append(xs)
        aliases = {2: 0}
    return pl.pallas_call(
        _dispatch_kernel,
        grid=(t // tm,),
        in_specs=in_specs,
        out_specs=pl.BlockSpec(memory_space=pl.ANY),
        out_shape=jax.ShapeDtypeStruct((n_slots, d), F32),
        scratch_shapes=[pltpu.SemaphoreType.DMA(())],
        input_output_aliases=aliases,
        compiler_params=pltpu.CompilerParams(dimension_semantics=("arbitrary",), has_side_effects=True),
        name="dispatch",
    )(*args)


def _expert_kernel(be_ref, nv_ref, xs_ref, wgu_ref, bgu_ref, wd_ref, bd_ref, ys_ref):
    i = pl.program_id(0)
    r = xs_ref.shape[0]
    nv = nv_ref[i]

    @pl.when(nv > 0)
    def _():
        rows = lax.broadcasted_iota(jnp.int32, (r, 1), 0)
        x = jnp.where(rows < nv, xs_ref[...], 0.0).astype(BF16)
        h = _dot(x, wgu_ref[0]) + bgu_ref[0]
        gate = jnp.minimum(h[:, :D_FF], SWIGLU_LIMIT)
        up = jnp.clip(h[:, D_FF:], -SWIGLU_LIMIT, SWIGLU_LIMIT)
        act = gate * jax.nn.sigmoid(SWIGLU_ALPHA * gate) * (up + 1.0)
        ys_ref[...] = _dot(act.astype(BF16), wd_ref[0]) + bd_ref[0]


def _experts(block_e, n_valid, xs, wgu, bgu, wd, bd, rows_per_block):
    n_slots, d = xs.shape
    r = rows_per_block
    n_blocks = n_slots // r
    grid_spec = pltpu.PrefetchScalarGridSpec(
        num_scalar_prefetch=2,
        grid=(n_blocks,),
        in_specs=[pl.BlockSpec((r, d), lambda i, be, nv: (i, 0)),
                  pl.BlockSpec((1, d, 2 * D_FF), lambda i, be, nv: (be[i], 0, 0)),
                  pl.BlockSpec((1, 1, 2 * D_FF), lambda i, be, nv: (be[i], 0, 0)),
                  pl.BlockSpec((1, D_FF, d), lambda i, be, nv: (be[i], 0, 0)),
                  pl.BlockSpec((1, 1, d), lambda i, be, nv: (be[i], 0, 0))],
        out_specs=pl.BlockSpec((r, d), lambda i, be, nv: (i, 0)),
    )
    return pl.pallas_call(
        _expert_kernel,
        grid_spec=grid_spec,
        out_shape=jax.ShapeDtypeStruct((n_slots, d), F32),
        compiler_params=pltpu.CompilerParams(dimension_semantics=("arbitrary",), vmem_limit_bytes=VMEM_LIMIT),
        name="experts",
    )(block_e, n_valid, xs, wgu, bgu, wd, bd)


def _combine_kernel(pos_ref, gate_ref, h_ref, gfin_ref, ys_ref, out_ref, buf_ref, sem):
    tm = h_ref.shape[0]

    def row_copy(t, k):
        return pltpu.make_async_copy(ys_ref.at[pl.ds(pos_ref[k, t], 1)], buf_ref.at[k, pl.ds(t, 1)], sem)

    def issue(t, c):
        for k in range(TOP_K):
            row_copy(t, k).start()
        return c

    lax.fori_loop(0, tm, issue, 0)

    def drain(t, c):
        for k in range(TOP_K):
            row_copy(t, k).wait()
        return c

    lax.fori_loop(0, tm, drain, 0)

    eye = lax.broadcasted_iota(jnp.int32, (tm, tm), 0) == lax.broadcasted_iota(jnp.int32, (tm, tm), 1)
    acc = h_ref[...]
    for k in range(TOP_K):
        g_col = jnp.sum(jnp.where(eye, gate_ref[k:k + 1, :], 0.0), axis=1, keepdims=True)
        acc = acc + g_col * buf_ref[k]
    out_ref[...] = _rms(acc, gfin_ref[...], RMS_EPS)


def _combine(pos, gate, h, g_final, ys):
    t, d = h.shape
    tm = min(256, t)
    assert t % tm == 0
    return pl.pallas_call(
        _combine_kernel,
        grid=(t // tm,),
        in_specs=[pl.BlockSpec((TOP_K, tm), lambda i: (0, i), memory_space=pltpu.SMEM),
                  pl.BlockSpec((TOP_K, tm), lambda i: (0, i)),
                  pl.BlockSpec((tm, d), lambda i: (i, 0)),
                  pl.BlockSpec(g_final.shape, lambda i: (0, 0)),
                  pl.BlockSpec(memory_space=pl.ANY)],
        out_specs=pl.BlockSpec((tm, d), lambda i: (i, 0)),
        out_shape=jax.ShapeDtypeStruct((t, d), F32),
        scratch_shapes=[pltpu.VMEM((TOP_K, tm, d), F32), pltpu.SemaphoreType.DMA(())],
        compiler_params=pltpu.CompilerParams(dimension_semantics=("arbitrary",), vmem_limit_bytes=VMEM_LIMIT),
        name="combine",
    )(pos, gate, h, g_final, ys)


def _rope_tables(pos):
    half = DIFF_D // 2
    inv = ROPE_THETA ** (-jnp.arange(half, dtype=F32) / half)
    ang = pos.astype(F32)[:, None] * inv[None, :]
    cos, sin = jnp.cos(ang), jnp.sin(ang)
    reps = LANES // DIFF_D
    return (jnp.tile(jnp.concatenate([cos, cos], axis=1), (1, reps)),
            jnp.tile(jnp.concatenate([-sin, sin], axis=1), (1, reps)))


def _prep_weights(w_in, g_attn_norm, g_q_a, w_uq, g_kv_a, w_uk, w_uv, w_o_mla, g_subln, w_o_diff, w_out,
                  g_ffn_norm, router_w, router_b):
    o_q, o_kv, o_kr, o_dq, o_dk, o_dv, o_g = 0, 256, 384, 416, 928, 1440, 1952
    w_a = jnp.concatenate([w_in[:, o_q:o_kv], w_in[:, o_kv:o_kr], w_in[:, o_dq:o_dk], w_in[:, o_dk:o_dv],
                           w_in[:, o_dv:o_g], w_in[:, o_kr:o_dq],
                           jnp.zeros((D_MODEL, LANES - MLA_ROPE), w_in.dtype)], axis=1).astype(BF16)
    w_g = w_in[:, o_g:].astype(BF16)
    uq = w_uq.reshape(Q_LORA, MLA_H, MLA_QK)
    w_uq_p = jnp.concatenate([uq[:, :, :MLA_NOPE].reshape(Q_LORA, MLA_H * MLA_NOPE),
                              uq[:, :, MLA_NOPE:].reshape(Q_LORA, MLA_H * MLA_ROPE)], axis=1).astype(BF16)
    ukt = jnp.transpose(w_uk, (1, 2, 0))
    w_n = jnp.zeros((MLA_H, MLA_H, MLA_NOPE, QK_PAD), F32)
    w_n = w_n.at[jnp.arange(MLA_H), jnp.arange(MLA_H), :, :KV_LORA].set(ukt)
    w_n = w_n.reshape(MLA_H, MLA_H * MLA_NOPE, QK_PAD).astype(BF16)
    eye = jnp.eye(MLA_ROPE, dtype=F32)
    w_e = jnp.zeros((MLA_H, MLA_H, MLA_ROPE, QK_PAD), F32)
    w_e = w_e.at[jnp.arange(MLA_H), jnp.arange(MLA_H), :, KV_LORA:KV_LORA + MLA_ROPE].set(
        jnp.broadcast_to(eye, (MLA_H, MLA_ROPE, MLA_ROPE)))
    w_e = w_e.reshape(MLA_H, MLA_H * MLA_ROPE, QK_PAD).astype(BF16)
    uv = jnp.transpose(w_uv, (1, 0, 2))
    w_uv_bd = jnp.zeros((MLA_H, KV_LORA, MLA_H, MLA_V), F32)
    w_uv_bd = w_uv_bd.at[jnp.arange(MLA_H), :, jnp.arange(MLA_H), :].set(uv)
    w_uv_bd = w_uv_bd.reshape(MLA_H * KV_LORA, MLA_H * MLA_V).astype(BF16)
    grp = jnp.arange(DIFF_W) // (2 * DIFF_D)
    bd64 = jnp.where(grp[:, None] == grp[None, :], 1.0 / (2 * DIFF_D), 0.0).astype(BF16)
    return {
        'g_attn': g_attn_norm.reshape(1, D_MODEL), 'w_a': w_a, 'w_g': w_g,
        'g_q_a': g_q_a.reshape(1, Q_LORA), 'w_uq': w_uq_p, 'w_n': w_n, 'w_e': w_e,
        'g_kv_a': g_kv_a.reshape(1, KV_LORA), 'w_uv_bd': w_uv_bd, 'w_o_mla': w_o_mla.astype(BF16),
        'bd64': bd64, 'g_subln': jnp.tile(g_subln, DIFF_H).reshape(1, DIFF_W),
        'w_o_diff': w_o_diff.astype(BF16), 'w_out': w_out.astype(BF16),
        'g_ffn': g_ffn_norm.reshape(1, D_MODEL), 'rw_t': router_w.T.astype(BF16),
        'rb': jnp.broadcast_to(router_b.reshape(N_EXPERTS, 1), (N_EXPERTS, LANES)),
    }


def _pad_keys(a, tk):
    lk = a.shape[1]
    pad = (-lk) % tk
    if pad == 0:
        return a
    return jnp.concatenate([a, jnp.zeros((a.shape[0], pad) + a.shape[2:], a.dtype)], axis=1)


def kernel(x_prompt, x_sample, cache_mla_ckv, cache_mla_kpe, cache_diff_k, cache_diff_v, g_attn_norm, w_in, g_q_a, w_uq, g_kv_a, w_uk, w_uv, w_o_mla, lambda_q1, lambda_k1, lambda_q2, lambda_k2, g_subln, w_o_diff, w_out, g_ffn_norm, router_w, router_b, w_gate_up, b_gate_up, w_down, b_down, g_final):
    depth = w_in.shape[0]
    assert depth == 1
    li = 0
    lam_init = 0.8 - 0.6 * math.exp(-0.3 * li)
    bp, lp, d = x_prompt.shape
    bs, ls, _ = x_sample.shape
    past = cache_mla_ckv.shape[2]
    tk = 256

    w = _prep_weights(w_in[li], g_attn_norm[li], g_q_a[li], w_uq[li], g_kv_a[li], w_uk[li], w_uv[li], w_o_mla[li],
                      g_subln[li], w_o_diff[li], w_out[li], g_ffn_norm[li], router_w[li], router_b[li])
    lam_rows = jnp.stack([lambda_q1[li], lambda_k1[li], lambda_q2[li], lambda_k2[li]]).astype(F32)

    cos_p, sin_p = _rope_tables(jnp.arange(lp, dtype=jnp.int32))
    ckv_p, kpe_p, dk_p, dv_p, qm_p, kvm_p, dqb_p, dkb_p, dvb_p = _proj(x_prompt, cos_p, sin_p, w)
    olat_p, odiff_p = _attention(qm_p, dqb_p, kvm_p, dkb_p, dvb_p, lam_rows, q_pos0=0, n_keys=lp, lam_init=lam_init)

    cos_s, sin_s = _rope_tables(jnp.arange(past, past + ls, dtype=jnp.int32))
    ckv_s, kpe_s, dk_s, dv_s, qm_s, kvm_s, dqb_s, dkb_s, dvb_s = _proj(x_sample, cos_s, sin_s, w)
    one = jnp.ones((bs, past, 1), F32)
    kvm_past = jnp.concatenate([cache_mla_ckv[li], cache_mla_kpe[li], one,
                                jnp.zeros((bs, past, QK_PAD - ONE_COL - 1), F32)], axis=2).astype(BF16)
    kvm_all = _pad_keys(jnp.concatenate([kvm_past, kvm_s], axis=1), tk)
    dkb_all = _pad_keys(jnp.concatenate([cache_diff_k[li].reshape(bs, past, DIFF_W).astype(BF16), dkb_s], axis=1), tk)
    dvb_all = _pad_keys(jnp.concatenate([cache_diff_v[li].reshape(bs, past, DIFF_W).astype(BF16), dvb_s], axis=1), tk)
    olat_s, odiff_s = _attention(qm_s, dqb_s, kvm_all, dkb_all, dvb_all, lam_rows,
                                 q_pos0=past, n_keys=past + ls, lam_init=lam_init)

    tp, ts = bp * lp, bs * ls
    cnt0 = jnp.zeros((N_EXPERTS, LANES), F32)
    h_p, tok_p, topi_p, rank_p, gate_p, cnt1 = _post(x_prompt.reshape(tp, d), olat_p.reshape(tp, -1),
                                                     odiff_p.reshape(tp, -1), cnt0, w, lam_init=lam_init)
    h_s, tok_s, topi_s, rank_s, gate_s, cnt2 = _post(x_sample.reshape(ts, d), olat_s.reshape(ts, -1),
                                                     odiff_s.reshape(ts, -1), cnt1, w, lam_init=lam_init)

    r = 512
    n_assign = (tp + ts) * TOP_K
    n_blocks = -(-n_assign // r) + N_EXPERTS
    n_slots = n_blocks * r
    counts = cnt2[:, 0].astype(jnp.int32)
    padded = (counts + r - 1) // r * r
    pend = jnp.cumsum(padded)
    pstart = pend - padded
    blk_start = jnp.arange(n_blocks, dtype=jnp.int32) * r
    block_e = jnp.minimum(jnp.searchsorted(pend, blk_start, side='right'), N_EXPERTS - 1).astype(jnp.int32)
    n_valid = jnp.clip(pstart[block_e] + counts[block_e] - blk_start, 0, r).astype(jnp.int32)
    n_valid = jnp.where(blk_start < pend[-1], n_valid, 0)
    pos_p = rank_p + pstart[topi_p]
    pos_s = rank_s + pstart[topi_s]

    xs = _dispatch(pos_p, tok_p, None, n_slots)
    xs = _dispatch(pos_s, tok_s, xs, n_slots)
    ys = _experts(block_e, n_valid, xs, w_gate_up[li].astype(BF16), b_gate_up[li].reshape(N_EXPERTS, 1, -1),
                  w_down[li].astype(BF16), b_down[li].reshape(N_EXPERTS, 1, -1), r)
    gfin = g_final.reshape(1, d)
    y_p = _combine(pos_p, gate_p, h_p, gfin, ys)
    y_s = _combine(pos_s, gate_s, h_s, gfin, ys)

    def stack(a, shape):
        return a.reshape((1,) + shape)

    return (y_p.reshape(bp, lp, d), y_s.reshape(bs, ls, d),
            stack(ckv_p, (bp, lp, KV_LORA)), stack(kpe_p, (bp, lp, MLA_ROPE)),
            stack(dk_p, (bp, lp, DIFF_H, 2, DIFF_D)), stack(dv_p, (bp, lp, DIFF_H, 2 * DIFF_D)),
            stack(ckv_s, (bs, ls, KV_LORA)), stack(kpe_s, (bs, ls, MLA_ROPE)),
            stack(dk_s, (bs, ls, DIFF_H, 2, DIFF_D)), stack(dv_s, (bs, ls, DIFF_H, 2 * DIFF_D)))
```

```python
import functools
import math

import jax
import jax.numpy as jnp
from jax import lax
from jax.experimental import pallas as pl
from jax.experimental.pallas import tpu as pltpu

F32 = jnp.float32
BF16 = jnp.bfloat16

D_MODEL = 1024
CHUNK = 64
CHUNK_SHIFT = 6
ROPE_THETA = 10000.0
RMS_EPS = 1e-6
SUBLN_EPS = 1e-5
NEG_INF = -1e30

MLA_H = 8
Q_LORA = 256
KV_LORA = 128
MLA_NOPE = 64
MLA_ROPE = 32
MLA_QK = MLA_NOPE + MLA_ROPE
MLA_V = 64
MLA_SCALE = MLA_QK ** -0.5

DIFF_H = 8
DIFF_D = 32
DIFF_W = DIFF_H * 2 * DIFF_D
DIFF_SCALE = DIFF_D ** -0.5
N_PAIR = DIFF_W // 128
MAPS_PER_PAIR = 4

N_EXPERTS = 32
TOP_K = 4
D_FF = D_MODEL
SWIGLU_ALPHA = 1.702
SWIGLU_LIMIT = 7.0

LANES = 128
QK_PAD = 256
LOG2E = math.log2(math.e)
VMEM_LIMIT = 56 * 1024 * 1024
N_DMA_PRIORITIES = 2

A_QD = 0
A_KV = A_QD + Q_LORA
A_DQ = A_KV + KV_LORA
A_DK = A_DQ + DIFF_W
A_DV = A_DK + DIFF_W
A_KR = A_DV + DIFF_W
A_COLS = A_KR + LANES


def _rms(x, g, eps):
    return x * lax.rsqrt(jnp.mean(x * x, axis=-1, keepdims=True) + eps) * g


def _dot(a, b):
    return jnp.dot(a, b, preferred_element_type=F32)


def _dot_nt(a, b):
    return lax.dot_general(a, b, (((1,), (1,)), ((), ())), preferred_element_type=F32)


def _proj_kernel(x_ref, cos_ref, sin_ref, gattn_ref, wa_ref, gqa_ref, wuq_ref, wn_ref, we_ref, gkva_ref,
                 ckv_ref, kpe_ref, dk_ref, dv_ref, qm_ref, kvm_ref, dqb_ref, dkb_ref, dvb_ref):
    tm = x_ref.shape[1]
    x = x_ref[0]
    xn = _rms(x, gattn_ref[...], RMS_EPS).astype(BF16)
    p = _dot(xn, wa_ref[...])

    cos = cos_ref[...]
    sin = sin_ref[...]
    lane = lax.broadcasted_iota(jnp.int32, (tm, LANES), 1)
    first_half = (lane & (DIFF_D - 1)) < (DIFF_D // 2)

    def rope128(xc):
        fwd = pltpu.roll(xc, LANES - DIFF_D // 2, 1)
        bwd = pltpu.roll(xc, DIFF_D // 2, 1)
        return xc * cos + jnp.where(first_half, fwd, bwd) * sin

    def rope_wide(xw):
        return jnp.concatenate([rope128(xw[:, c * LANES:(c + 1) * LANES]) for c in range(xw.shape[1] // LANES)], axis=1)

    cq = _rms(p[:, A_QD:A_QD + Q_LORA], gqa_ref[...], RMS_EPS).astype(BF16)
    q = _dot(cq, wuq_ref[...])
    qn = q[:, :MLA_H * MLA_NOPE].astype(BF16)
    qr = rope_wide(q[:, MLA_H * MLA_NOPE:]).astype(BF16)
    for h in range(MLA_H):
        qe = _dot(qn, wn_ref[h]) + _dot(qr, we_ref[h])
        qm_ref[0, h] = (qe * (MLA_SCALE * LOG2E)).astype(BF16)

    ckv = _rms(p[:, A_KV:A_KV + KV_LORA], gkva_ref[...], RMS_EPS)
    ckv_ref[0] = ckv
    kseg = rope128(p[:, A_KR:A_KR + LANES])
    kpe_ref[0] = kseg[:, :MLA_ROPE]
    kvm_ref[0] = jnp.concatenate([ckv, kseg], axis=1).astype(BF16)

    dq = rope_wide(p[:, A_DQ:A_DQ + DIFF_W])
    dqb_ref[0] = (dq * (DIFF_SCALE * LOG2E)).astype(BF16)
    dk = rope_wide(p[:, A_DK:A_DK + DIFF_W])
    dk_ref[0] = dk
    dkb_ref[0] = dk.astype(BF16)
    dv = p[:, A_DV:A_DV + DIFF_W]
    dv_ref[0] = dv
    dvb_ref[0] = dv.astype(BF16)


def _proj(x, cos, sin, w):
    b, l, d = x.shape
    tm = min(512, l)
    assert l % tm == 0
    grid = (l // tm, b)

    def full(a):
        return pl.BlockSpec(a.shape, lambda i, j, n=a.ndim: (0,) * n)

    def tok(width):
        return pl.BlockSpec((1, tm, width), lambda i, j: (j, i, 0))

    weights = (w['g_attn'], w['w_a'], w['g_q_a'], w['w_uq'], w['w_n'], w['w_e'], w['g_kv_a'])
    out_shape = (
        jax.ShapeDtypeStruct((b, l, KV_LORA), F32),
        jax.ShapeDtypeStruct((b, l, MLA_ROPE), F32),
        jax.ShapeDtypeStruct((b, l, DIFF_W), F32),
        jax.ShapeDtypeStruct((b, l, DIFF_W), F32),
        jax.ShapeDtypeStruct((b, MLA_H, l, QK_PAD), BF16),
        jax.ShapeDtypeStruct((b, l, QK_PAD), BF16),
        jax.ShapeDtypeStruct((b, l, DIFF_W), BF16),
        jax.ShapeDtypeStruct((b, l, DIFF_W), BF16),
        jax.ShapeDtypeStruct((b, l, DIFF_W), BF16),
    )
    out_specs = (tok(KV_LORA), tok(MLA_ROPE), tok(DIFF_W), tok(DIFF_W),
                 pl.BlockSpec((1, MLA_H, tm, QK_PAD), lambda i, j: (j, 0, i, 0)),
                 tok(QK_PAD), tok(DIFF_W), tok(DIFF_W), tok(DIFF_W))
    tab = pl.BlockSpec((tm, LANES), lambda i, j: (i, 0))
    return pl.pallas_call(
        _proj_kernel,
        grid=grid,
        in_specs=[tok(d), tab, tab] + [full(a) for a in weights],
        out_specs=out_specs,
        out_shape=out_shape,
        compiler_params=pltpu.CompilerParams(dimension_semantics=("arbitrary", "arbitrary"),
                                             vmem_limit_bytes=VMEM_LIMIT),
        name="proj",
    )(x, cos, sin, *weights)


def _attn_kernel(qm_ref, dq_ref, kvm_ref, dk_ref, dv_ref, lam_ref, olat_ref, odiff_ref,
                 qd_ref, m_ref, l_ref, acc_ref, *, tq, tk, q_pos0, n_keys, lam_init):
    i = pl.program_id(1)
    n_blocks = MLA_H + MAPS_PER_PAIR * N_PAIR

    lane = lax.broadcasted_iota(jnp.int32, (tq, LANES), 1)
    seg = lax.shift_right_logical(lane, 5)
    for p in range(N_PAIR):
        qp = dq_ref[0, :, p * LANES:(p + 1) * LANES].astype(F32)
        for j in range(MAPS_PER_PAIR):
            blk = MAPS_PER_PAIR * p + j
            qd_ref[blk * tq:(blk + 1) * tq] = jnp.where(seg == j, qp, 0.0).astype(BF16)

    m_ref[...] = jnp.full(m_ref.shape, NEG_INF, F32)
    l_ref[...] = jnp.zeros(l_ref.shape, F32)
    acc_ref[...] = jnp.zeros(acc_ref.shape, F32)

    q_lo = q_pos0 + i * tq
    q_hi = q_lo + tq - 1
    vis_all = jnp.minimum((q_lo // CHUNK + 1) * CHUNK, n_keys)
    vis_any = jnp.minimum((q_hi // CHUNK + 1) * CHUNK, n_keys)
    n_full = vis_all // tk
    n_tot = (vis_any + tk - 1) // tk
    q_chunk = lax.shift_right_logical(q_lo + lax.broadcasted_iota(jnp.int32, (tq, 1), 0), CHUNK_SHIFT)

    def step(kt, masked):
        start = pl.multiple_of(kt * tk, tk)
        if masked:
            k_pos = kt * tk + lax.broadcasted_iota(jnp.int32, (1, tk), 1)
            k_chunk = jnp.where(k_pos < n_keys, lax.shift_right_logical(k_pos, CHUNK_SHIFT), jnp.int32(2 ** 30))
            visible = k_chunk <= q_chunk
        for blk in range(n_blocks):
            rows = slice(blk * tq, (blk + 1) * tq)
            if blk < MLA_H:
                q = qm_ref[0, blk]
                k = kvm_ref[0, pl.ds(start, tk), :]
                v = kvm_ref[0, pl.ds(start, tk), 0:KV_LORA]
            else:
                p = (blk - MLA_H) // MAPS_PER_PAIR
                q = qd_ref[(blk - MLA_H) * tq:(blk - MLA_H + 1) * tq]
                k = dk_ref[0, pl.ds(start, tk), p * LANES:(p + 1) * LANES]
                v = dv_ref[0, pl.ds(start, tk), p * LANES:(p + 1) * LANES]
            s = _dot_nt(q, k)
            if masked:
                s = jnp.where(visible, s, NEG_INF)
            m_old = m_ref[rows]
            m_new = jnp.maximum(m_old, jnp.max(s, axis=1, keepdims=True))
            alpha = jnp.exp2(m_old - m_new)
            ps = [jnp.exp2(s[:, c * LANES:(c + 1) * LANES] - m_new) for c in range(tk // LANES)]
            l_ref[rows] = alpha * l_ref[rows] + sum(ps)
            acc_ref[rows] = alpha * acc_ref[rows] + _dot(jnp.concatenate(ps, axis=1).astype(BF16), v)
            m_ref[rows] = m_new

    def full_step(kt, c):
        step(kt, False)
        return c

    def masked_step(kt, c):
        step(kt, True)
        return c

    lax.fori_loop(0, n_full, full_step, 0)
    lax.fori_loop(n_full, n_tot, masked_step, 0)

    def normalised(blk):
        rows = slice(blk * tq, (blk + 1) * tq)
        return acc_ref[rows] / jnp.sum(l_ref[rows], axis=1, keepdims=True)

    for h in range(MLA_H):
        olat_ref[0, :, h * LANES:(h + 1) * LANES] = normalised(h).astype(BF16)

    lam = (jnp.exp(jnp.sum(lam_ref[0:1] * lam_ref[1:2], axis=1, keepdims=True))
           - jnp.exp(jnp.sum(lam_ref[2:3] * lam_ref[3:4], axis=1, keepdims=True)) + lam_init)
    for p in range(N_PAIR):
        n = [normalised(MLA_H + MAPS_PER_PAIR * p + j) for j in range(MAPS_PER_PAIR)]
        head0 = n[0] - lam * n[1]
        head1 = n[2] - lam * n[3]
        odiff_ref[0, :, p * LANES:(p + 1) * LANES] = jnp.where(lane < 2 * DIFF_D, head0, head1)


def _attention(qm, dqb, kvm, dkb, dvb, lam_rows, *, q_pos0, n_keys, lam_init):
    b, _, l, _ = qm.shape
    lk = kvm.shape[1]
    tq = min(256, l)
    tk = 256
    assert l % tq == 0 and lk % tk == 0 and tq & (tq - 1) == 0
    n_rows = (MLA_H + MAPS_PER_PAIR * N_PAIR) * tq
    kern = functools.partial(_attn_kernel, tq=tq, tk=tk, q_pos0=q_pos0, n_keys=n_keys, lam_init=lam_init)
    per_batch = lambda width: pl.BlockSpec((1, lk, width), lambda j, i: (j, 0, 0))
    return pl.pallas_call(
        kern,
        grid=(b, l // tq),
        in_specs=[pl.BlockSpec((1, MLA_H, tq, QK_PAD), lambda j, i: (j, 0, i, 0)),
                  pl.BlockSpec((1, tq, DIFF_W), lambda j, i: (j, i, 0)),
                  per_batch(QK_PAD), per_batch(DIFF_W), per_batch(DIFF_W),
                  pl.BlockSpec(lam_rows.shape, lambda j, i: (0, 0))],
        out_specs=(pl.BlockSpec((1, tq, MLA_H * KV_LORA), lambda j, i: (j, i, 0)),
                   pl.BlockSpec((1, tq, DIFF_W), lambda j, i: (j, i, 0))),
        out_shape=(jax.ShapeDtypeStruct((b, l, MLA_H * KV_LORA), BF16),
                   jax.ShapeDtypeStruct((b, l, DIFF_W), F32)),
        scratch_shapes=[pltpu.VMEM((MAPS_PER_PAIR * N_PAIR * tq, LANES), BF16),
                        pltpu.VMEM((n_rows, LANES), F32),
                        pltpu.VMEM((n_rows, LANES), F32),
                        pltpu.VMEM((n_rows, LANES), F32)],
        compiler_params=pltpu.CompilerParams(dimension_semantics=("arbitrary", "arbitrary"),
                                             vmem_limit_bytes=VMEM_LIMIT),
        name="attn",
    )(qm, dqb, kvm, dkb, dvb, lam_rows)


def _post_kernel(x_ref, olat_ref, odiff_ref, cnt_in_ref, su_ref, gattn_ref, wg_ref, wuv_ref, womla_ref, bd_ref,
                 gsub_ref, wodiff_ref, wout_ref, gffn_ref, rwt_ref, rb_ref,
                 h_ref, tok_ref, topi_ref, rank_ref, gate_ref, cnt_ref, *, lam_init):
    tm = x_ref.shape[0]
    x = x_ref[...]
    xn = _rms(x, gattn_ref[...], RMS_EPS).astype(BF16)
    g = _dot(xn, wg_ref[...])
    y_mla = _dot(_dot(olat_ref[...], wuv_ref[...]).astype(BF16), womla_ref[...])

    od = odiff_ref[...]
    sq = od * od
    sq_hi = sq.astype(BF16)
    sq_lo = (sq - sq_hi.astype(F32)).astype(BF16)
    ms = _dot(sq_hi, bd_ref[...]) + _dot(sq_lo, bd_ref[...])
    on = od * lax.rsqrt(ms + SUBLN_EPS) * gsub_ref[...] * (1.0 - lam_init)
    y_diff = _dot(on.astype(BF16), wodiff_ref[...])

    merged = jax.nn.sigmoid(g[:, :D_MODEL]) * y_mla + jax.nn.sigmoid(g[:, D_MODEL:]) * y_diff
    h = x + _dot(merged.astype(BF16), wout_ref[...])
    h_ref[...] = h
    tok = _rms(h, gffn_ref[...], RMS_EPS)
    tok_ref[...] = tok

    logits = _dot_nt(rwt_ref[...], tok.astype(BF16)) + rb_ref[:, 0:1]
    e_iota = lax.broadcasted_iota(jnp.int32, (N_EXPERTS, tm), 0).astype(F32)
    work = logits
    vals, hots = [], []
    for k in range(TOP_K):
        mx = jnp.max(work, axis=0, keepdims=True)
        idx = jnp.min(jnp.where(work == mx, e_iota, float(N_EXPERTS)), axis=0, keepdims=True)
        hot = e_iota == idx
        work = jnp.where(hot, -jnp.inf, work)
        vals.append(mx)
        hots.append(hot)
        topi_ref[k:k + 1, :] = idx.astype(jnp.int32)
    exps = [jnp.exp(v - vals[0]) for v in vals]
    den = exps[0] + exps[1] + exps[2] + exps[3]
    for k in range(TOP_K):
        gate_ref[k:k + 1, :] = exps[k] / den

    @pl.when(pl.program_id(0) == 0)
    def _():
        cnt_ref[...] = cnt_in_ref[...]

    sel = sum(jnp.where(hot, 1.0, 0.0) for hot in hots)
    before = _dot(sel.astype(BF16), su_ref[...]) + cnt_ref[:, 0:1]
    for k in range(TOP_K):
        rank_ref[k:k + 1, :] = jnp.sum(jnp.where(hots[k], before, 0.0), axis=0, keepdims=True).astype(jnp.int32)
    cnt_ref[...] = cnt_ref[...] + jnp.sum(sel, axis=1, keepdims=True)


def _post(x2d, olat2d, odiff2d, cnt_in, w, *, lam_init):
    t, d = x2d.shape
    tm = min(512, t)
    assert t % tm == 0
    su = (lax.broadcasted_iota(jnp.int32, (tm, tm), 0) < lax.broadcasted_iota(jnp.int32, (tm, tm), 1)).astype(BF16)
    weights = (w['g_attn'], w['w_g'], w['w_uv_bd'], w['w_o_mla'], w['bd64'], w['g_subln'], w['w_o_diff'],
               w['w_out'], w['g_ffn'], w['rw_t'], w['rb'])

    def full(a):
        return pl.BlockSpec(a.shape, lambda i, n=a.ndim: (0,) * n)

    tokspec = lambda width: pl.BlockSpec((tm, width), lambda i: (i, 0))
    small = pl.BlockSpec((TOP_K, tm), lambda i: (0, i))
    kern = functools.partial(_post_kernel, lam_init=lam_init)
    return pl.pallas_call(
        kern,
        grid=(t // tm,),
        in_specs=[tokspec(d), tokspec(MLA_H * KV_LORA), tokspec(DIFF_W), full(cnt_in), full(su)]
                 + [full(a) for a in weights],
        out_specs=(tokspec(d), tokspec(d), small, small, small, full(cnt_in)),
        out_shape=(jax.ShapeDtypeStruct((t, d), F32), jax.ShapeDtypeStruct((t, d), F32),
                   jax.ShapeDtypeStruct((TOP_K, t), jnp.int32), jax.ShapeDtypeStruct((TOP_K, t), jnp.int32),
                   jax.ShapeDtypeStruct((TOP_K, t), F32), jax.ShapeDtypeStruct(cnt_in.shape, F32)),
        compiler_params=pltpu.CompilerParams(dimension_semantics=("arbitrary",), vmem_limit_bytes=VMEM_LIMIT),
        name="post",
    )(x2d, olat2d, odiff2d, cnt_in, su, *weights)


def _dispatch_kernel(pos_ref, tok_ref, *rest):
    xs_ref, sems = rest[-2], rest[-1]
    tm = tok_ref.shape[0]

    def issue(t, c):
        for k in range(TOP_K):
            q = k % N_DMA_PRIORITIES
            pltpu.make_async_copy(tok_ref.at[pl.ds(t, 1)], xs_ref.at[pl.ds(pos_ref[k, t], 1)],
                                  sems.at[q]).start(priority=q)
        return c

    lax.fori_loop(0, tm, issue, 0, unroll=8)
    for k in range(TOP_K):
        pltpu.make_async_copy(tok_ref, xs_ref.at[pl.ds(0, tm)], sems.at[k % N_DMA_PRIORITIES]).wait()


def _dispatch(pos, tok, xs, n_slots):
    t, d = tok.shape
    tm = min(256, t)
    assert t % tm == 0
    in_specs = [pl.BlockSpec((TOP_K, tm), lambda i: (0, i), memory_space=pltpu.SMEM),
                pl.BlockSpec((tm, d), lambda i: (i, 0))]
    args = [pos, tok]
    aliases = {}
    if xs is not None:
        in_specs.append(pl.BlockSpec(memory_space=pl.ANY))
        args.append(xs)
        aliases = {2: 0}
    return pl.pallas_call(
        _dispatch_kernel,
        grid=(t // tm,),
        in_specs=in_specs,
        out_specs=pl.BlockSpec(memory_space=pl.ANY),
        out_shape=jax.ShapeDtypeStruct((n_slots, d), F32),
        scratch_shapes=[pltpu.SemaphoreType.DMA((N_DMA_PRIORITIES,))],
        input_output_aliases=aliases,
        compiler_params=pltpu.CompilerParams(dimension_semantics=("arbitrary",), has_side_effects=True),
        name="dispatch",
    )(*args)


def _expert_kernel(be_ref, nv_ref, xs_ref, wgu_ref, bgu_ref, wd_ref, bd_ref, ys_ref):
    i = pl.program_id(0)
    r = xs_ref.shape[0]
    nv = nv_ref[i]

    @pl.when(nv > 0)
    def _():
        rows = lax.broadcasted_iota(jnp.int32, (r, 1), 0)
        x = jnp.where(rows < nv, xs_ref[...], 0.0).astype(BF16)
        h = _dot(x, wgu_ref[0]) + bgu_ref[0]
        gate = jnp.minimum(h[:, :D_FF], SWIGLU_LIMIT)
        up = jnp.clip(h[:, D_FF:], -SWIGLU_LIMIT, SWIGLU_LIMIT)
        act = gate * jax.nn.sigmoid(SWIGLU_ALPHA * gate) * (up + 1.0)
        ys_ref[...] = _dot(act.astype(BF16), wd_ref[0]) + bd_ref[0]


def _experts(block_e, n_valid, xs, wgu, bgu, wd, bd, rows_per_block):
    n_slots, d = xs.shape
    r = rows_per_block
    n_blocks = n_slots // r
    grid_spec = pltpu.PrefetchScalarGridSpec(
        num_scalar_prefetch=2,
        grid=(n_blocks,),
        in_specs=[pl.BlockSpec((r, d), lambda i, be, nv: (i, 0)),
                  pl.BlockSpec((1, d, 2 * D_FF), lambda i, be, nv: (be[i], 0, 0)),
                  pl.BlockSpec((1, 1, 2 * D_FF), lambda i, be, nv: (be[i], 0, 0)),
                  pl.BlockSpec((1, D_FF, d), lambda i, be, nv: (be[i], 0, 0)),
                  pl.BlockSpec((1, 1, d), lambda i, be, nv: (be[i], 0, 0))],
        out_specs=pl.BlockSpec((r, d), lambda i, be, nv: (i, 0)),
    )
    return pl.pallas_call(
        _expert_kernel,
        grid_spec=grid_spec,
        out_shape=jax.ShapeDtypeStruct((n_slots, d), F32),
        compiler_params=pltpu.CompilerParams(dimension_semantics=("arbitrary",), vmem_limit_bytes=VMEM_LIMIT),
        name="experts",
    )(block_e, n_valid, xs, wgu, bgu, wd, bd)


def _combine_kernel(pos_ref, gate_ref, h_ref, gfin_ref, ys_ref, out_ref, buf_ref, sems):
    tm = h_ref.shape[0]

    def issue(t, c):
        for k in range(TOP_K):
            q = k % N_DMA_PRIORITIES
            pltpu.make_async_copy(ys_ref.at[pl.ds(pos_ref[k, t], 1)], buf_ref.at[k, pl.ds(t, 1)],
                                  sems.at[q]).start(priority=q)
        return c

    lax.fori_loop(0, tm, issue, 0, unroll=8)
    for k in range(TOP_K):
        pltpu.make_async_copy(ys_ref.at[pl.ds(0, tm)], buf_ref.at[k], sems.at[k % N_DMA_PRIORITIES]).wait()

    eye = lax.broadcasted_iota(jnp.int32, (tm, tm), 0) == lax.broadcasted_iota(jnp.int32, (tm, tm), 1)
    acc = h_ref[...]
    for k in range(TOP_K):
        g_col = jnp.sum(jnp.where(eye, gate_ref[k:k + 1, :], 0.0), axis=1, keepdims=True)
        acc = acc + g_col * buf_ref[k]
    out_ref[...] = _rms(acc, gfin_ref[...], RMS_EPS)


def _combine(pos, gate, h, g_final, ys):
    t, d = h.shape
    tm = min(256, t)
    assert t % tm == 0
    return pl.pallas_call(
        _combine_kernel,
        grid=(t // tm,),
        in_specs=[pl.BlockSpec((TOP_K, tm), lambda i: (0, i), memory_space=pltpu.SMEM),
                  pl.BlockSpec((TOP_K, tm), lambda i: (0, i)),
                  pl.BlockSpec((tm, d), lambda i: (i, 0)),
                  pl.BlockSpec(g_final.shape, lambda i: (0, 0)),
                  pl.BlockSpec(memory_space=pl.ANY)],
        out_specs=pl.BlockSpec((tm, d), lambda i: (i, 0)),
        out_shape=jax.ShapeDtypeStruct((t, d), F32),
        scratch_shapes=[pltpu.VMEM((TOP_K, tm, d), F32), pltpu.SemaphoreType.DMA((N_DMA_PRIORITIES,))],
        compiler_params=pltpu.CompilerParams(dimension_semantics=("arbitrary",), vmem_limit_bytes=VMEM_LIMIT),
        name="combine",
    )(pos, gate, h, g_final, ys)


def _rope_tables(pos):
    half = DIFF_D // 2
    inv = ROPE_THETA ** (-jnp.arange(half, dtype=F32) / half)
    ang = pos.astype(F32)[:, None] * inv[None, :]
    cos, sin = jnp.cos(ang), jnp.sin(ang)
    reps = LANES // DIFF_D
    return (jnp.tile(jnp.concatenate([cos, cos], axis=1), (1, reps)),
            jnp.tile(jnp.concatenate([-sin, sin], axis=1), (1, reps)))


def _prep_weights(w_in, g_attn_norm, g_q_a, w_uq, g_kv_a, w_uk, w_uv, w_o_mla, g_subln, w_o_diff, w_out,
                  g_ffn_norm, router_w, router_b):
    o_q, o_kv, o_kr, o_dq, o_dk, o_dv, o_g = 0, 256, 384, 416, 928, 1440, 1952
    w_a = jnp.concatenate([w_in[:, o_q:o_kv], w_in[:, o_kv:o_kr], w_in[:, o_dq:o_dk], w_in[:, o_dk:o_dv],
                           w_in[:, o_dv:o_g], w_in[:, o_kr:o_dq],
                           jnp.zeros((D_MODEL, LANES - MLA_ROPE), w_in.dtype)], axis=1).astype(BF16)
    w_g = w_in[:, o_g:].astype(BF16)
    uq = w_uq.reshape(Q_LORA, MLA_H, MLA_QK)
    w_uq_p = jnp.concatenate([uq[:, :, :MLA_NOPE].reshape(Q_LORA, MLA_H * MLA_NOPE),
                              uq[:, :, MLA_NOPE:].reshape(Q_LORA, MLA_H * MLA_ROPE)], axis=1).astype(BF16)
    ukt = jnp.transpose(w_uk, (1, 2, 0))
    w_n = jnp.zeros((MLA_H, MLA_H, MLA_NOPE, QK_PAD), F32)
    w_n = w_n.at[jnp.arange(MLA_H), jnp.arange(MLA_H), :, :KV_LORA].set(ukt)
    w_n = w_n.reshape(MLA_H, MLA_H * MLA_NOPE, QK_PAD).astype(BF16)
    eye = jnp.eye(MLA_ROPE, dtype=F32)
    w_e = jnp.zeros((MLA_H, MLA_H, MLA_ROPE, QK_PAD), F32)
    w_e = w_e.at[jnp.arange(MLA_H), jnp.arange(MLA_H), :, KV_LORA:KV_LORA + MLA_ROPE].set(
        jnp.broadcast_to(eye, (MLA_H, MLA_ROPE, MLA_ROPE)))
    w_e = w_e.reshape(MLA_H, MLA_H * MLA_ROPE, QK_PAD).astype(BF16)
    uv = jnp.transpose(w_uv, (1, 0, 2))
    w_uv_bd = jnp.zeros((MLA_H, KV_LORA, MLA_H, MLA_V), F32)
    w_uv_bd = w_uv_bd.at[jnp.arange(MLA_H), :, jnp.arange(MLA_H), :].set(uv)
    w_uv_bd = w_uv_bd.reshape(MLA_H * KV_LORA, MLA_H * MLA_V).astype(BF16)
    grp = jnp.arange(DIFF_W) // (2 * DIFF_D)
    bd64 = jnp.where(grp[:, None] == grp[None, :], 1.0 / (2 * DIFF_D), 0.0).astype(BF16)
    return {
        'g_attn': g_attn_norm.reshape(1, D_MODEL), 'w_a': w_a, 'w_g': w_g,
        'g_q_a': g_q_a.reshape(1, Q_LORA), 'w_uq': w_uq_p, 'w_n': w_n, 'w_e': w_e,
        'g_kv_a': g_kv_a.reshape(1, KV_LORA), 'w_uv_bd': w_uv_bd, 'w_o_mla': w_o_mla.astype(BF16),
        'bd64': bd64, 'g_subln': jnp.tile(g_subln, DIFF_H).reshape(1, DIFF_W),
        'w_o_diff': w_o_diff.astype(BF16), 'w_out': w_out.astype(BF16),
        'g_ffn': g_ffn_norm.reshape(1, D_MODEL), 'rw_t': router_w.T.astype(BF16),
        'rb': jnp.broadcast_to(router_b.reshape(N_EXPERTS, 1), (N_EXPERTS, LANES)),
    }


def _pad_keys(a, tk):
    lk = a.shape[1]
    pad = (-lk) % tk
    if pad == 0:
        return a
    return jnp.concatenate([a, jnp.zeros((a.shape[0], pad) + a.shape[2:], a.dtype)], axis=1)


def _slot_base(topi, pstart):
    e = jnp.arange(N_EXPERTS, dtype=jnp.int32)
    return jnp.sum(jnp.where(topi[..., None] == e, pstart, 0), axis=-1)


def kernel(x_prompt, x_sample, cache_mla_ckv, cache_mla_kpe, cache_diff_k, cache_diff_v, g_attn_norm, w_in, g_q_a, w_uq, g_kv_a, w_uk, w_uv, w_o_mla, lambda_q1, lambda_k1, lambda_q2, lambda_k2, g_subln, w_o_diff, w_out, g_ffn_norm, router_w, router_b, w_gate_up, b_gate_up, w_down, b_down, g_final):
    depth = w_in.shape[0]
    assert depth == 1
    li = 0
    lam_init = 0.8 - 0.6 * math.exp(-0.3 * li)
    bp, lp, d = x_prompt.shape
    bs, ls, _ = x_sample.shape
    past = cache_mla_ckv.shape[2]
    tk = 256

    w = _prep_weights(w_in[li], g_attn_norm[li], g_q_a[li], w_uq[li], g_kv_a[li], w_uk[li], w_uv[li], w_o_mla[li],
                      g_subln[li], w_o_diff[li], w_out[li], g_ffn_norm[li], router_w[li], router_b[li])
    lam_rows = jnp.stack([lambda_q1[li], lambda_k1[li], lambda_q2[li], lambda_k2[li]]).astype(F32)

    cos_p, sin_p = _rope_tables(jnp.arange(lp, dtype=jnp.int32))
    ckv_p, kpe_p, dk_p, dv_p, qm_p, kvm_p, dqb_p, dkb_p, dvb_p = _proj(x_prompt, cos_p, sin_p, w)
    olat_p, odiff_p = _attention(qm_p, dqb_p, kvm_p, dkb_p, dvb_p, lam_rows, q_pos0=0, n_keys=lp, lam_init=lam_init)

    cos_s, sin_s = _rope_tables(jnp.arange(past, past + ls, dtype=jnp.int32))
    ckv_s, kpe_s, dk_s, dv_s, qm_s, kvm_s, dqb_s, dkb_s, dvb_s = _proj(x_sample, cos_s, sin_s, w)
    kvm_past = jnp.concatenate([cache_mla_ckv[li], cache_mla_kpe[li],
                                jnp.zeros((bs, past, QK_PAD - KV_LORA - MLA_ROPE), F32)], axis=2).astype(BF16)
    kvm_all = _pad_keys(jnp.concatenate([kvm_past, kvm_s], axis=1), tk)
    dkb_all = _pad_keys(jnp.concatenate([cache_diff_k[li].reshape(bs, past, DIFF_W).astype(BF16), dkb_s], axis=1), tk)
    dvb_all = _pad_keys(jnp.concatenate([cache_diff_v[li].reshape(bs, past, DIFF_W).astype(BF16), dvb_s], axis=1), tk)
    olat_s, odiff_s = _attention(qm_s, dqb_s, kvm_all, dkb_all, dvb_all, lam_rows,
                                 q_pos0=past, n_keys=past + ls, lam_init=lam_init)

    tp, ts = bp * lp, bs * ls
    cnt0 = jnp.zeros((N_EXPERTS, LANES), F32)
    h_p, tok_p, topi_p, rank_p, gate_p, cnt1 = _post(x_prompt.reshape(tp, d), olat_p.reshape(tp, -1),
                                                     odiff_p.reshape(tp, -1), cnt0, w, lam_init=lam_init)
    h_s, tok_s, topi_s, rank_s, gate_s, cnt2 = _post(x_sample.reshape(ts, d), olat_s.reshape(ts, -1),
                                                     odiff_s.reshape(ts, -1), cnt1, w, lam_init=lam_init)

    r = 512
    n_assign = (tp + ts) * TOP_K
    n_blocks = -(-n_assign // r) + N_EXPERTS
    n_slots = n_blocks * r
    counts = cnt2[:, 0].astype(jnp.int32)
    padded = (counts + r - 1) // r * r
    pend = jnp.cumsum(padded)
    pstart = pend - padded
    blk_start = jnp.arange(n_blocks, dtype=jnp.int32) * r
    block_e = jnp.minimum(jnp.sum((pend[None, :] <= blk_start[:, None]).astype(jnp.int32), axis=1), N_EXPERTS - 1)
    seg_end = _slot_base(block_e, pstart + counts)
    n_valid = jnp.where(blk_start < pend[-1], jnp.clip(seg_end - blk_start, 0, r), 0).astype(jnp.int32)
    pos_p = rank_p + _slot_base(topi_p, pstart)
    pos_s = rank_s + _slot_base(topi_s, pstart)

    xs = _dispatch(pos_p, tok_p, None, n_slots)
    xs = _dispatch(pos_s, tok_s, xs, n_slots)
    ys = _experts(block_e, n_valid, xs, w_gate_up[li].astype(BF16), b_gate_up[li].reshape(N_EXPERTS, 1, -1),
                  w_down[li].astype(BF16), b_down[li].reshape(N_EXPERTS, 1, -1), r)
    gfin = g_final.reshape(1, d)
    y_p = _combine(pos_p, gate_p, h_p, gfin, ys)
    y_s = _combine(pos_s, gate_s, h_s, gfin, ys)

    def stack(a, shape):
        return a.reshape((1,) + shape)

    return (y_p.reshape(bp, lp, d), y_s.reshape(bs, ls, d),
            stack(ckv_p, (bp, lp, KV_LORA)), stack(kpe_p, (bp, lp, MLA_ROPE)),
            stack(dk_p, (bp, lp, DIFF_H, 2, DIFF_D)), stack(dv_p, (bp, lp, DIFF_H, 2 * DIFF_D)),
            stack(ckv_s, (bs, ls, KV_LORA)), stack(kpe_s, (bs, ls, MLA_ROPE)),
            stack(dk_s, (bs, ls, DIFF_H, 2, DIFF_D)), stack(dv_s, (bs, ls, DIFF_H, 2 * DIFF_D)))
```

```python
import functools
import math

import jax
import jax.numpy as jnp
from jax import lax
from jax.experimental import pallas as pl
from jax.experimental.pallas import tpu as pltpu

F32 = jnp.float32
BF16 = jnp.bfloat16

D_MODEL = 1024
CHUNK = 64
CHUNK_SHIFT = 6
ROPE_THETA = 10000.0
RMS_EPS = 1e-6
SUBLN_EPS = 1e-5
NEG_INF = -1e30

MLA_H = 8
Q_LORA = 256
KV_LORA = 128
MLA_NOPE = 64
MLA_ROPE = 32
MLA_QK = MLA_NOPE + MLA_ROPE
MLA_V = 64
MLA_SCALE = MLA_QK ** -0.5

DIFF_H = 8
DIFF_D = 32
DIFF_W = DIFF_H * 2 * DIFF_D
DIFF_SCALE = DIFF_D ** -0.5
N_PAIR = DIFF_W // 128
MAPS_PER_PAIR = 4

N_EXPERTS = 32
TOP_K = 4
D_FF = D_MODEL
SWIGLU_ALPHA = 1.702
SWIGLU_LIMIT = 7.0

LANES = 128
QK_PAD = 256
LOG2E = math.log2(math.e)
VMEM_LIMIT = 56 * 1024 * 1024
N_DMA_PRIORITIES = 2
MERGE_ROWS = 256

A_QD = 0
A_KV = A_QD + Q_LORA
A_DQ = A_KV + KV_LORA
A_DK = A_DQ + DIFF_W
A_DV = A_DK + DIFF_W
A_KR = A_DV + DIFF_W
A_COLS = A_KR + LANES


def _rms(x, g, eps):
    return x * lax.rsqrt(jnp.mean(x * x, axis=-1, keepdims=True) + eps) * g


def _dot(a, b):
    return jnp.dot(a, b, preferred_element_type=F32)


def _dot_nt(a, b):
    return lax.dot_general(a, b, (((1,), (1,)), ((), ())), preferred_element_type=F32)


def _proj_kernel(x_ref, cos_ref, sin_ref, gattn_ref, wa_ref, gqa_ref, wuq_ref, wn_ref, we_ref, gkva_ref,
                 ckv_ref, kpe_ref, dk_ref, dv_ref, qm_ref, kvm_ref, dqb_ref, dkb_ref, dvb_ref):
    tm = x_ref.shape[1]
    x = x_ref[0]
    xn = _rms(x, gattn_ref[...], RMS_EPS).astype(BF16)
    p = _dot(xn, wa_ref[...])

    cos = cos_ref[...]
    sin = sin_ref[...]
    lane = lax.broadcasted_iota(jnp.int32, (tm, LANES), 1)
    first_half = (lane & (DIFF_D - 1)) < (DIFF_D // 2)

    def rope128(xc):
        fwd = pltpu.roll(xc, LANES - DIFF_D // 2, 1)
        bwd = pltpu.roll(xc, DIFF_D // 2, 1)
        return xc * cos + jnp.where(first_half, fwd, bwd) * sin

    def rope_wide(xw):
        return jnp.concatenate([rope128(xw[:, c * LANES:(c + 1) * LANES]) for c in range(xw.shape[1] // LANES)], axis=1)

    cq = _rms(p[:, A_QD:A_QD + Q_LORA], gqa_ref[...], RMS_EPS).astype(BF16)
    q = _dot(cq, wuq_ref[...])
    qn = q[:, :MLA_H * MLA_NOPE].astype(BF16)
    qr = rope_wide(q[:, MLA_H * MLA_NOPE:]).astype(BF16)
    q_lat = _dot(qn, wn_ref[...])
    q_pe = _dot(qr, we_ref[...])
    for h in range(MLA_H):
        cols = slice(h * KV_LORA, (h + 1) * KV_LORA)
        qe = jnp.concatenate([q_lat[:, cols], q_pe[:, cols]], axis=1)
        qm_ref[0, h] = (qe * (MLA_SCALE * LOG2E)).astype(BF16)

    ckv = _rms(p[:, A_KV:A_KV + KV_LORA], gkva_ref[...], RMS_EPS)
    ckv_ref[0] = ckv
    kseg = rope128(p[:, A_KR:A_KR + LANES])
    kpe_ref[0] = kseg[:, :MLA_ROPE]
    kvm_ref[0] = jnp.concatenate([ckv, kseg], axis=1).astype(BF16)

    dq = rope_wide(p[:, A_DQ:A_DQ + DIFF_W])
    dqb_ref[0] = (dq * (DIFF_SCALE * LOG2E)).astype(BF16)
    dk = rope_wide(p[:, A_DK:A_DK + DIFF_W])
    dk_ref[0] = dk
    dkb_ref[0] = dk.astype(BF16)
    dv = p[:, A_DV:A_DV + DIFF_W]
    dv_ref[0] = dv
    dvb_ref[0] = dv.astype(BF16)


def _proj(x, cos, sin, w):
    b, l, d = x.shape
    tm = min(512, l)
    assert l % tm == 0
    grid = (l // tm, b)

    def full(a):
        return pl.BlockSpec(a.shape, lambda i, j, n=a.ndim: (0,) * n)

    def tok(width):
        return pl.BlockSpec((1, tm, width), lambda i, j: (j, i, 0))

    weights = (w['g_attn'], w['w_a'], w['g_q_a'], w['w_uq'], w['w_n'], w['w_e'], w['g_kv_a'])
    out_shape = (
        jax.ShapeDtypeStruct((b, l, KV_LORA), F32),
        jax.ShapeDtypeStruct((b, l, MLA_ROPE), F32),
        jax.ShapeDtypeStruct((b, l, DIFF_W), F32),
        jax.ShapeDtypeStruct((b, l, DIFF_W), F32),
        jax.ShapeDtypeStruct((b, MLA_H, l, QK_PAD), BF16),
        jax.ShapeDtypeStruct((b, l, QK_PAD), BF16),
        jax.ShapeDtypeStruct((b, l, DIFF_W), BF16),
        jax.ShapeDtypeStruct((b, l, DIFF_W), BF16),
        jax.ShapeDtypeStruct((b, l, DIFF_W), BF16),
    )
    out_specs = (tok(KV_LORA), tok(MLA_ROPE), tok(DIFF_W), tok(DIFF_W),
                 pl.BlockSpec((1, MLA_H, tm, QK_PAD), lambda i, j: (j, 0, i, 0)),
                 tok(QK_PAD), tok(DIFF_W), tok(DIFF_W), tok(DIFF_W))
    tab = pl.BlockSpec((tm, LANES), lambda i, j: (i, 0))
    return pl.pallas_call(
        _proj_kernel,
        grid=grid,
        in_specs=[tok(d), tab, tab] + [full(a) for a in weights],
        out_specs=out_specs,
        out_shape=out_shape,
        compiler_params=pltpu.CompilerParams(dimension_semantics=("arbitrary", "arbitrary"),
                                             vmem_limit_bytes=VMEM_LIMIT),
        name="proj",
    )(x, cos, sin, *weights)


def _attn_kernel(qm_ref, dq_ref, kvm_ref, dk_ref, dv_ref, lam_ref, olat_ref, odiff_ref,
                 qd_ref, m_ref, l_ref, acc_ref, *, tq, tk, q_pos0, n_keys, lam_init):
    i = pl.program_id(1)

    lane = lax.broadcasted_iota(jnp.int32, (tq, LANES), 1)
    seg = lax.shift_right_logical(lane, 5)
    for p in range(N_PAIR):
        qp = dq_ref[0, :, p * LANES:(p + 1) * LANES].astype(F32)
        for j in range(MAPS_PER_PAIR):
            blk = MAPS_PER_PAIR * p + j
            qd_ref[blk * tq:(blk + 1) * tq] = jnp.where(seg == j, qp, 0.0).astype(BF16)

    m_ref[...] = jnp.full(m_ref.shape, NEG_INF, F32)
    l_ref[...] = jnp.zeros(l_ref.shape, F32)
    acc_ref[...] = jnp.zeros(acc_ref.shape, F32)

    q_lo = q_pos0 + i * tq
    q_hi = q_lo + tq - 1
    vis_all = jnp.minimum((q_lo // CHUNK + 1) * CHUNK, n_keys)
    vis_any = jnp.minimum((q_hi // CHUNK + 1) * CHUNK, n_keys)
    n_full = vis_all // tk
    n_tot = (vis_any + tk - 1) // tk

    merge = max(1, MERGE_ROWS // tq)
    g_mla, g_diff = min(MLA_H, merge), min(MAPS_PER_PAIR, merge)
    work = [(b, g_mla) for b in range(0, MLA_H, g_mla)]
    work += [(MLA_H + b, g_diff) for b in range(0, MAPS_PER_PAIR * N_PAIR, g_diff)]

    def step(start, width, masked):
        if masked:
            k_pos = start + lax.broadcasted_iota(jnp.int32, (1, width), 1)
            k_chunk = jnp.where(k_pos < n_keys, lax.shift_right_logical(k_pos, CHUNK_SHIFT), jnp.int32(2 ** 30))
        for blk, g in work:
            n_rows = g * tq
            rows = slice(blk * tq, blk * tq + n_rows)
            if blk < MLA_H:
                q = qm_ref[0, blk] if g == 1 else qm_ref[0, blk:blk + g].reshape(n_rows, QK_PAD)
                k = kvm_ref[0, pl.ds(start, width), :]
                v = kvm_ref[0, pl.ds(start, width), 0:KV_LORA]
            else:
                p = (blk - MLA_H) // MAPS_PER_PAIR
                q = qd_ref[(blk - MLA_H) * tq:(blk - MLA_H) * tq + n_rows]
                k = dk_ref[0, pl.ds(start, width), p * LANES:(p + 1) * LANES]
                v = dv_ref[0, pl.ds(start, width), p * LANES:(p + 1) * LANES]
            s = _dot_nt(q, k)
            if masked:
                q_row = lax.broadcasted_iota(jnp.int32, (n_rows, 1), 0) & (tq - 1)
                q_chunk = lax.shift_right_logical(q_lo + q_row, CHUNK_SHIFT)
                s = jnp.where(k_chunk <= q_chunk, s, NEG_INF)
            m_old = m_ref[rows]
            m_new = jnp.maximum(m_old, jnp.max(s, axis=1, keepdims=True))
            alpha = jnp.exp2(m_old - m_new)
            ps = [jnp.exp2(s[:, c * LANES:(c + 1) * LANES] - m_new) for c in range(width // LANES)]
            l_ref[rows] = alpha * l_ref[rows] + sum(ps)
            acc_ref[rows] = alpha * acc_ref[rows] + _dot(jnp.concatenate(ps, axis=1).astype(BF16), v)
            m_ref[rows] = m_new

    def full_step(kt, c):
        step(pl.multiple_of(kt * tk, tk), tk, False)
        return c

    def masked_step(kt, c):
        step(pl.multiple_of(kt * tk, tk), tk, True)
        return c

    lax.fori_loop(0, n_full, full_step, 0)
    lax.fori_loop(n_full, n_tot, masked_step, 0)

    def normalised(blk):
        rows = slice(blk * tq, (blk + 1) * tq)
        return acc_ref[rows] / jnp.sum(l_ref[rows], axis=1, keepdims=True)

    for h in range(MLA_H):
        olat_ref[0, :, h * LANES:(h + 1) * LANES] = normalised(h).astype(BF16)

    lam = (jnp.exp(jnp.sum(lam_ref[0:1] * lam_ref[1:2], axis=1, keepdims=True))
           - jnp.exp(jnp.sum(lam_ref[2:3] * lam_ref[3:4], axis=1, keepdims=True)) + lam_init)
    for p in range(N_PAIR):
        n = [normalised(MLA_H + MAPS_PER_PAIR * p + j) for j in range(MAPS_PER_PAIR)]
        head0 = n[0] - lam * n[1]
        head1 = n[2] - lam * n[3]
        odiff_ref[0, :, p * LANES:(p + 1) * LANES] = jnp.where(lane < 2 * DIFF_D, head0, head1)


def _attention(qm, dqb, kvm, dkb, dvb, lam_rows, *, q_pos0, n_keys, lam_init):
    b, _, l, _ = qm.shape
    lk = kvm.shape[1]
    tq = min(256, l)
    tk = 256
    assert l % tq == 0 and lk % tk == 0 and tq & (tq - 1) == 0
    n_rows = (MLA_H + MAPS_PER_PAIR * N_PAIR) * tq
    kern = functools.partial(_attn_kernel, tq=tq, tk=tk, q_pos0=q_pos0, n_keys=n_keys, lam_init=lam_init)
    per_batch = lambda width: pl.BlockSpec((1, lk, width), lambda j, i: (j, 0, 0))
    return pl.pallas_call(
        kern,
        grid=(b, l // tq),
        in_specs=[pl.BlockSpec((1, MLA_H, tq, QK_PAD), lambda j, i: (j, 0, i, 0)),
                  pl.BlockSpec((1, tq, DIFF_W), lambda j, i: (j, i, 0)),
                  per_batch(QK_PAD), per_batch(DIFF_W), per_batch(DIFF_W),
                  pl.BlockSpec(lam_rows.shape, lambda j, i: (0, 0))],
        out_specs=(pl.BlockSpec((1, tq, MLA_H * KV_LORA), lambda j, i: (j, i, 0)),
                   pl.BlockSpec((1, tq, DIFF_W), lambda j, i: (j, i, 0))),
        out_shape=(jax.ShapeDtypeStruct((b, l, MLA_H * KV_LORA), BF16),
                   jax.ShapeDtypeStruct((b, l, DIFF_W), F32)),
        scratch_shapes=[pltpu.VMEM((MAPS_PER_PAIR * N_PAIR * tq, LANES), BF16),
                        pltpu.VMEM((n_rows, LANES), F32),
                        pltpu.VMEM((n_rows, LANES), F32),
                        pltpu.VMEM((n_rows, LANES), F32)],
        compiler_params=pltpu.CompilerParams(dimension_semantics=("arbitrary", "arbitrary"),
                                             vmem_limit_bytes=VMEM_LIMIT),
        name="attn",
    )(qm, dqb, kvm, dkb, dvb, lam_rows)


def _post_kernel(x_ref, olat_ref, odiff_ref, cnt_in_ref, su_ref, gattn_ref, wg_ref, wuv_ref, womla_ref, bd_ref,
                 gsub_ref, wodiff_ref, wout_ref, gffn_ref, rwt_ref, rb_ref,
                 h_ref, tok_ref, topi_ref, rank_ref, gate_ref, cnt_ref, *, lam_init):
    tm = x_ref.shape[0]
    x = x_ref[...]
    xn = _rms(x, gattn_ref[...], RMS_EPS).astype(BF16)
    g = _dot(xn, wg_ref[...])
    y_mla = _dot(_dot(olat_ref[...], wuv_ref[...]).astype(BF16), womla_ref[...])

    od = odiff_ref[...]
    sq = od * od
    sq_hi = sq.astype(BF16)
    sq_lo = (sq - sq_hi.astype(F32)).astype(BF16)
    ms = _dot(sq_hi, bd_ref[...]) + _dot(sq_lo, bd_ref[...])
    on = od * lax.rsqrt(ms + SUBLN_EPS) * gsub_ref[...] * (1.0 - lam_init)
    y_diff = _dot(on.astype(BF16), wodiff_ref[...])

    merged = jax.nn.sigmoid(g[:, :D_MODEL]) * y_mla + jax.nn.sigmoid(g[:, D_MODEL:]) * y_diff
    h = x + _dot(merged.astype(BF16), wout_ref[...])
    h_ref[...] = h
    tok = _rms(h, gffn_ref[...], RMS_EPS)
    tok_ref[...] = tok

    logits = _dot_nt(rwt_ref[...], tok.astype(BF16)) + rb_ref[:, 0:1]
    e_iota = lax.broadcasted_iota(jnp.int32, (N_EXPERTS, tm), 0).astype(F32)
    work = logits
    vals, hots = [], []
    for k in range(TOP_K):
        mx = jnp.max(work, axis=0, keepdims=True)
        idx = jnp.min(jnp.where(work == mx, e_iota, float(N_EXPERTS)), axis=0, keepdims=True)
        hot = e_iota == idx
        work = jnp.where(hot, -jnp.inf, work)
        vals.append(mx)
        hots.append(hot)
        topi_ref[k:k + 1, :] = idx.astype(jnp.int32)
    exps = [jnp.exp(v - vals[0]) for v in vals]
    den = exps[0] + exps[1] + exps[2] + exps[3]
    for k in range(TOP_K):
        gate_ref[k:k + 1, :] = exps[k] / den

    @pl.when(pl.program_id(0) == 0)
    def _():
        cnt_ref[...] = cnt_in_ref[...]

    sel = sum(jnp.where(hot, 1.0, 0.0) for hot in hots)
    before = _dot(sel.astype(BF16), su_ref[...]) + cnt_ref[:, 0:1]
    for k in range(TOP_K):
        rank_ref[k:k + 1, :] = jnp.sum(jnp.where(hots[k], before, 0.0), axis=0, keepdims=True).astype(jnp.int32)
    cnt_ref[...] = cnt_ref[...] + jnp.sum(sel, axis=1, keepdims=True)


def _post(x2d, olat2d, odiff2d, cnt_in, w, *, lam_init):
    t, d = x2d.shape
    tm = min(512, t)
    assert t % tm == 0
    su = (lax.broadcasted_iota(jnp.int32, (tm, tm), 0) < lax.broadcasted_iota(jnp.int32, (tm, tm), 1)).astype(BF16)
    weights = (w['g_attn'], w['w_g'], w['w_uv_bd'], w['w_o_mla'], w['bd64'], w['g_subln'], w['w_o_diff'],
               w['w_out'], w['g_ffn'], w['rw_t'], w['rb'])

    def full(a):
        return pl.BlockSpec(a.shape, lambda i, n=a.ndim: (0,) * n)

    tokspec = lambda width: pl.BlockSpec((tm, width), lambda i: (i, 0))
    small = pl.BlockSpec((TOP_K, tm), lambda i: (0, i))
    kern = functools.partial(_post_kernel, lam_init=lam_init)
    return pl.pallas_call(
        kern,
        grid=(t // tm,),
        in_specs=[tokspec(d), tokspec(MLA_H * KV_LORA), tokspec(DIFF_W), full(cnt_in), full(su)]
                 + [full(a) for a in weights],
        out_specs=(tokspec(d), tokspec(d), small, small, small, full(cnt_in)),
        out_shape=(jax.ShapeDtypeStruct((t, d), F32), jax.ShapeDtypeStruct((t, d), F32),
                   jax.ShapeDtypeStruct((TOP_K, t), jnp.int32), jax.ShapeDtypeStruct((TOP_K, t), jnp.int32),
                   jax.ShapeDtypeStruct((TOP_K, t), F32), jax.ShapeDtypeStruct(cnt_in.shape, F32)),
        compiler_params=pltpu.CompilerParams(dimension_semantics=("arbitrary",), vmem_limit_bytes=VMEM_LIMIT),
        name="post",
    )(x2d, olat2d, odiff2d, cnt_in, su, *weights)


def _dispatch_kernel(pos_ref, tok_ref, *rest):
    xs_ref, sems = rest[-2], rest[-1]
    tm = tok_ref.shape[0]

    for t in range(tm):
        for k in range(TOP_K):
            q = k % N_DMA_PRIORITIES
            pltpu.make_async_copy(tok_ref.at[pl.ds(t, 1)], xs_ref.at[pl.ds(pos_ref[k, t], 1)],
                                  sems.at[q]).start(priority=q)
    for k in range(TOP_K):
        pltpu.make_async_copy(tok_ref, xs_ref.at[pl.ds(0, tm)], sems.at[k % N_DMA_PRIORITIES]).wait()


def _dispatch(pos, tok, xs, n_slots):
    t, d = tok.shape
    tm = min(256, t)
    assert t % tm == 0
    in_specs = [pl.BlockSpec((TOP_K, tm), lambda i: (0, i), memory_space=pltpu.SMEM),
                pl.BlockSpec((tm, d), lambda i: (i, 0))]
    args = [pos, tok]
    aliases = {}
    if xs is not None:
        in_specs.append(pl.BlockSpec(memory_space=pl.ANY))
        args.append(xs)
        aliases = {2: 0}
    return pl.pallas_call(
        _dispatch_kernel,
        grid=(t // tm,),
        in_specs=in_specs,
        out_specs=pl.BlockSpec(memory_space=pl.ANY),
        out_shape=jax.ShapeDtypeStruct((n_slots, d), F32),
        scratch_shapes=[pltpu.SemaphoreType.DMA((N_DMA_PRIORITIES,))],
        input_output_aliases=aliases,
        compiler_params=pltpu.CompilerParams(dimension_semantics=("arbitrary",), has_side_effects=True),
        name="dispatch",
    )(*args)


def _expert_kernel(be_ref, nv_ref, xs_ref, wgu_ref, bgu_ref, wd_ref, bd_ref, ys_ref):
    i = pl.program_id(0)
    r = xs_ref.shape[0]
    nv = nv_ref[i]

    @pl.when(nv > 0)
    def _():
        rows = lax.broadcasted_iota(jnp.int32, (r, 1), 0)
        x = jnp.where(rows < nv, xs_ref[...], 0.0).astype(BF16)
        h = _dot(x, wgu_ref[0]) + bgu_ref[0]
        gate = jnp.minimum(h[:, :D_FF], SWIGLU_LIMIT)
        up = jnp.clip(h[:, D_FF:], -SWIGLU_LIMIT, SWIGLU_LIMIT)
        act = gate * jax.nn.sigmoid(SWIGLU_ALPHA * gate) * (up + 1.0)
        ys_ref[...] = _dot(act.astype(BF16), wd_ref[0]) + bd_ref[0]


def _experts(block_e, n_valid, xs, wgu, bgu, wd, bd, rows_per_block):
    n_slots, d = xs.shape
    r = rows_per_block
    n_blocks = n_slots // r
    grid_spec = pltpu.PrefetchScalarGridSpec(
        num_scalar_prefetch=2,
        grid=(n_blocks,),
        in_specs=[pl.BlockSpec((r, d), lambda i, be, nv: (i, 0)),
                  pl.BlockSpec((1, d, 2 * D_FF), lambda i, be, nv: (be[i], 0, 0)),
                  pl.BlockSpec((1, 1, 2 * D_FF), lambda i, be, nv: (be[i], 0, 0)),
                  pl.BlockSpec((1, D_FF, d), lambda i, be, nv: (be[i], 0, 0)),
                  pl.BlockSpec((1, 1, d), lambda i, be, nv: (be[i], 0, 0))],
        out_specs=pl.BlockSpec((r, d), lambda i, be, nv: (i, 0)),
    )
    return pl.pallas_call(
        _expert_kernel,
        grid_spec=grid_spec,
        out_shape=jax.ShapeDtypeStruct((n_slots, d), F32),
        compiler_params=pltpu.CompilerParams(dimension_semantics=("arbitrary",), vmem_limit_bytes=VMEM_LIMIT),
        name="experts",
    )(block_e, n_valid, xs, wgu, bgu, wd, bd)


def _combine_kernel(pos_ref, gate_ref, h_ref, gfin_ref, ys_ref, out_ref, buf_ref, sems):
    tm = h_ref.shape[0]

    for t in range(tm):
        for k in range(TOP_K):
            q = k % N_DMA_PRIORITIES
            pltpu.make_async_copy(ys_ref.at[pl.ds(pos_ref[k, t], 1)], buf_ref.at[k, pl.ds(t, 1)],
                                  sems.at[q]).start(priority=q)
    for k in range(TOP_K):
        pltpu.make_async_copy(ys_ref.at[pl.ds(0, tm)], buf_ref.at[k], sems.at[k % N_DMA_PRIORITIES]).wait()

    eye = lax.broadcasted_iota(jnp.int32, (tm, tm), 0) == lax.broadcasted_iota(jnp.int32, (tm, tm), 1)
    acc = h_ref[...]
    for k in range(TOP_K):
        g_col = jnp.sum(jnp.where(eye, gate_ref[k:k + 1, :], 0.0), axis=1, keepdims=True)
        acc = acc + g_col * buf_ref[k]
    out_ref[...] = _rms(acc, gfin_ref[...], RMS_EPS)


def _combine(pos, gate, h, g_final, ys):
    t, d = h.shape
    tm = min(256, t)
    assert t % tm == 0
    return pl.pallas_call(
        _combine_kernel,
        grid=(t // tm,),
        in_specs=[pl.BlockSpec((TOP_K, tm), lambda i: (0, i), memory_space=pltpu.SMEM),
                  pl.BlockSpec((TOP_K, tm), lambda i: (0, i)),
                  pl.BlockSpec((tm, d), lambda i: (i, 0)),
                  pl.BlockSpec(g_final.shape, lambda i: (0, 0)),
                  pl.BlockSpec(memory_space=pl.ANY)],
        out_specs=pl.BlockSpec((tm, d), lambda i: (i, 0)),
        out_shape=jax.ShapeDtypeStruct((t, d), F32),
        scratch_shapes=[pltpu.VMEM((TOP_K, tm, d), F32), pltpu.SemaphoreType.DMA((N_DMA_PRIORITIES,))],
        compiler_params=pltpu.CompilerParams(dimension_semantics=("arbitrary",), vmem_limit_bytes=VMEM_LIMIT),
        name="combine",
    )(pos, gate, h, g_final, ys)


def _rope_tables(pos):
    half = DIFF_D // 2
    inv = ROPE_THETA ** (-jnp.arange(half, dtype=F32) / half)
    ang = pos.astype(F32)[:, None] * inv[None, :]
    cos, sin = jnp.cos(ang), jnp.sin(ang)
    reps = LANES // DIFF_D
    return (jnp.tile(jnp.concatenate([cos, cos], axis=1), (1, reps)),
            jnp.tile(jnp.concatenate([-sin, sin], axis=1), (1, reps)))


def _prep_weights(w_in, g_attn_norm, g_q_a, w_uq, g_kv_a, w_uk, w_uv, w_o_mla, g_subln, w_o_diff, w_out,
                  g_ffn_norm, router_w, router_b):
    o_q, o_kv, o_kr, o_dq, o_dk, o_dv, o_g = 0, 256, 384, 416, 928, 1440, 1952
    w_a = jnp.concatenate([w_in[:, o_q:o_kv], w_in[:, o_kv:o_kr], w_in[:, o_dq:o_dk], w_in[:, o_dk:o_dv],
                           w_in[:, o_dv:o_g], w_in[:, o_kr:o_dq],
                           jnp.zeros((D_MODEL, LANES - MLA_ROPE), w_in.dtype)], axis=1).astype(BF16)
    w_g = w_in[:, o_g:].astype(BF16)
    uq = w_uq.reshape(Q_LORA, MLA_H, MLA_QK)
    w_uq_p = jnp.concatenate([uq[:, :, :MLA_NOPE].reshape(Q_LORA, MLA_H * MLA_NOPE),
                              uq[:, :, MLA_NOPE:].reshape(Q_LORA, MLA_H * MLA_ROPE)], axis=1).astype(BF16)
    same_head = jnp.eye(MLA_H, dtype=bool)[:, None, :, None]
    ukt = jnp.transpose(w_uk, (1, 2, 0))
    w_n = jnp.where(same_head, ukt[:, :, None, :], 0.0).reshape(MLA_H * MLA_NOPE, MLA_H * KV_LORA).astype(BF16)
    place = jnp.eye(MLA_ROPE, KV_LORA, dtype=F32)
    w_e = jnp.where(same_head, place[None, :, None, :], 0.0).reshape(MLA_H * MLA_ROPE, MLA_H * KV_LORA).astype(BF16)
    uv = jnp.transpose(w_uv, (1, 0, 2))
    w_uv_bd = jnp.where(same_head, uv[:, :, None, :], 0.0).reshape(MLA_H * KV_LORA, MLA_H * MLA_V).astype(BF16)
    grp = jnp.arange(DIFF_W) // (2 * DIFF_D)
    bd64 = jnp.where(grp[:, None] == grp[None, :], 1.0 / (2 * DIFF_D), 0.0).astype(BF16)
    return {
        'g_attn': g_attn_norm.reshape(1, D_MODEL), 'w_a': w_a, 'w_g': w_g,
        'g_q_a': g_q_a.reshape(1, Q_LORA), 'w_uq': w_uq_p, 'w_n': w_n, 'w_e': w_e,
        'g_kv_a': g_kv_a.reshape(1, KV_LORA), 'w_uv_bd': w_uv_bd, 'w_o_mla': w_o_mla.astype(BF16),
        'bd64': bd64, 'g_subln': jnp.tile(g_subln, DIFF_H).reshape(1, DIFF_W),
        'w_o_diff': w_o_diff.astype(BF16), 'w_out': w_out.astype(BF16),
        'g_ffn': g_ffn_norm.reshape(1, D_MODEL), 'rw_t': router_w.T.astype(BF16),
        'rb': jnp.broadcast_to(router_b.reshape(N_EXPERTS, 1), (N_EXPERTS, LANES)),
    }


def _pad_keys(a, tk):
    lk = a.shape[1]
    pad = (-lk) % tk
    if pad == 0:
        return a
    return jnp.concatenate([a, jnp.zeros((a.shape[0], pad) + a.shape[2:], a.dtype)], axis=1)


def _slot_base(topi, pstart):
    e = jnp.arange(N_EXPERTS, dtype=jnp.int32)
    return jnp.sum(jnp.where(topi[..., None] == e, pstart, 0), axis=-1)


def kernel(x_prompt, x_sample, cache_mla_ckv, cache_mla_kpe, cache_diff_k, cache_diff_v, g_attn_norm, w_in, g_q_a, w_uq, g_kv_a, w_uk, w_uv, w_o_mla, lambda_q1, lambda_k1, lambda_q2, lambda_k2, g_subln, w_o_diff, w_out, g_ffn_norm, router_w, router_b, w_gate_up, b_gate_up, w_down, b_down, g_final):
    depth = w_in.shape[0]
    assert depth == 1
    li = 0
    lam_init = 0.8 - 0.6 * math.exp(-0.3 * li)
    bp, lp, d = x_prompt.shape
    bs, ls, _ = x_sample.shape
    past = cache_mla_ckv.shape[2]
    tk = 256

    w = _prep_weights(w_in[li], g_attn_norm[li], g_q_a[li], w_uq[li], g_kv_a[li], w_uk[li], w_uv[li], w_o_mla[li],
                      g_subln[li], w_o_diff[li], w_out[li], g_ffn_norm[li], router_w[li], router_b[li])
    lam_rows = jnp.stack([lambda_q1[li], lambda_k1[li], lambda_q2[li], lambda_k2[li]]).astype(F32)

    cos_p, sin_p = _rope_tables(jnp.arange(lp, dtype=jnp.int32))
    ckv_p, kpe_p, dk_p, dv_p, qm_p, kvm_p, dqb_p, dkb_p, dvb_p = _proj(x_prompt, cos_p, sin_p, w)
    olat_p, odiff_p = _attention(qm_p, dqb_p, kvm_p, dkb_p, dvb_p, lam_rows, q_pos0=0, n_keys=lp, lam_init=lam_init)

    cos_s, sin_s = _rope_tables(jnp.arange(past, past + ls, dtype=jnp.int32))
    ckv_s, kpe_s, dk_s, dv_s, qm_s, kvm_s, dqb_s, dkb_s, dvb_s = _proj(x_sample, cos_s, sin_s, w)
    kvm_past = jnp.concatenate([cache_mla_ckv[li], cache_mla_kpe[li],
                                jnp.zeros((bs, past, QK_PAD - KV_LORA - MLA_ROPE), F32)], axis=2).astype(BF16)
    kvm_all = _pad_keys(jnp.concatenate([kvm_past, kvm_s], axis=1), tk)
    dkb_all = _pad_keys(jnp.concatenate([cache_diff_k[li].reshape(bs, past, DIFF_W).astype(BF16), dkb_s], axis=1), tk)
    dvb_all = _pad_keys(jnp.concatenate([cache_diff_v[li].reshape(bs, past, DIFF_W).astype(BF16), dvb_s], axis=1), tk)
    olat_s, odiff_s = _attention(qm_s, dqb_s, kvm_all, dkb_all, dvb_all, lam_rows,
                                 q_pos0=past, n_keys=past + ls, lam_init=lam_init)

    tp, ts = bp * lp, bs * ls
    cnt0 = jnp.zeros((N_EXPERTS, LANES), F32)
    h_p, tok_p, topi_p, rank_p, gate_p, cnt1 = _post(x_prompt.reshape(tp, d), olat_p.reshape(tp, -1),
                                                     odiff_p.reshape(tp, -1), cnt0, w, lam_init=lam_init)
    h_s, tok_s, topi_s, rank_s, gate_s, cnt2 = _post(x_sample.reshape(ts, d), olat_s.reshape(ts, -1),
                                                     odiff_s.reshape(ts, -1), cnt1, w, lam_init=lam_init)

    r = 512
    n_assign = (tp + ts) * TOP_K
    n_blocks = -(-n_assign // r) + N_EXPERTS
    n_slots = n_blocks * r
    counts = cnt2[:, 0].astype(jnp.int32)
    padded = (counts + r - 1) // r * r
    pend = jnp.cumsum(padded)
    pstart = pend - padded
    blk_start = jnp.arange(n_blocks, dtype=jnp.int32) * r
    block_e = jnp.minimum(jnp.sum((pend[None, :] <= blk_start[:, None]).astype(jnp.int32), axis=1), N_EXPERTS - 1)
    seg_end = _slot_base(block_e, pstart + counts)
    n_valid = jnp.where(blk_start < pend[-1], jnp.clip(seg_end - blk_start, 0, r), 0).astype(jnp.int32)
    pos_p = rank_p + _slot_base(topi_p, pstart)
    pos_s = rank_s + _slot_base(topi_s, pstart)

    xs = _dispatch(pos_p, tok_p, None, n_slots)
    xs = _dispatch(pos_s, tok_s, xs, n_slots)
    ys = _experts(block_e, n_valid, xs, w_gate_up[li].astype(BF16), b_gate_up[li].reshape(N_EXPERTS, 1, -1),
                  w_down[li].astype(BF16), b_down[li].reshape(N_EXPERTS, 1, -1), r)
    gfin = g_final.reshape(1, d)
    y_p = _combine(pos_p, gate_p, h_p, gfin, ys)
    y_s = _combine(pos_s, gate_s, h_s, gfin, ys)

    def stack(a, shape):
        return a.reshape((1,) + shape)

    return (y_p.reshape(bp, lp, d), y_s.reshape(bs, ls, d),
            stack(ckv_p, (bp, lp, KV_LORA)), stack(kpe_p, (bp, lp, MLA_ROPE)),
            stack(dk_p, (bp, lp, DIFF_H, 2, DIFF_D)), stack(dv_p, (bp, lp, DIFF_H, 2 * DIFF_D)),
            stack(ckv_s, (bs, ls, KV_LORA)), stack(kpe_s, (bs, ls, MLA_ROPE)),
            stack(dk_s, (bs, ls, DIFF_H, 2, DIFF_D)), stack(dv_s, (bs, ls, DIFF_H, 2 * DIFF_D)))
```

```python
import functools
import math

import jax
import jax.numpy as jnp
from jax import lax
from jax.experimental import pallas as pl
from jax.experimental.pallas import tpu as pltpu

F32 = jnp.float32
BF16 = jnp.bfloat16

D_MODEL = 1024
CHUNK = 64
CHUNK_SHIFT = 6
ROPE_THETA = 10000.0
RMS_EPS = 1e-6
SUBLN_EPS = 1e-5
NEG_INF = -1e30

MLA_H = 8
Q_LORA = 256
KV_LORA = 128
MLA_NOPE = 64
MLA_ROPE = 32
MLA_QK = MLA_NOPE + MLA_ROPE
MLA_V = 64
MLA_SCALE = MLA_QK ** -0.5

DIFF_H = 8
DIFF_D = 32
DIFF_W = DIFF_H * 2 * DIFF_D
DIFF_SCALE = DIFF_D ** -0.5
N_PAIR = DIFF_W // 128
MAPS_PER_PAIR = 4

N_EXPERTS = 32
TOP_K = 4
D_FF = D_MODEL
SWIGLU_ALPHA = 1.702
SWIGLU_LIMIT = 7.0

LANES = 128
QK_PAD = 256
LOG2E = math.log2(math.e)
VMEM_LIMIT = 56 * 1024 * 1024
N_DMA_PRIORITIES = 2
MERGE_ROWS = 256

A_QD = 0
A_KV = A_QD + Q_LORA
A_DQ = A_KV + KV_LORA
A_DK = A_DQ + DIFF_W
A_DV = A_DK + DIFF_W
A_KR = A_DV + DIFF_W
A_COLS = A_KR + LANES


def _rms(x, g, eps):
    return x * lax.rsqrt(jnp.mean(x * x, axis=-1, keepdims=True) + eps) * g


def _dot(a, b):
    return jnp.dot(a, b, preferred_element_type=F32)


def _dot_nt(a, b):
    return lax.dot_general(a, b, (((1,), (1,)), ((), ())), preferred_element_type=F32)


def _proj_kernel(x_ref, cos_ref, sin_ref, gattn_ref, wa_ref, gqa_ref, wuq_ref, wn_ref, we_ref, gkva_ref,
                 ckv_ref, kpe_ref, dk_ref, dv_ref, qm_ref, kvm_ref, dqb_ref, dkb_ref, dvb_ref):
    tm = x_ref.shape[1]
    x = x_ref[0]
    xn = _rms(x, gattn_ref[...], RMS_EPS).astype(BF16)
    p = _dot(xn, wa_ref[...])

    cos = cos_ref[...]
    sin = sin_ref[...]
    lane = lax.broadcasted_iota(jnp.int32, (tm, LANES), 1)
    first_half = (lane & (DIFF_D - 1)) < (DIFF_D // 2)

    def rope128(xc):
        fwd = pltpu.roll(xc, LANES - DIFF_D // 2, 1)
        bwd = pltpu.roll(xc, DIFF_D // 2, 1)
        return xc * cos + jnp.where(first_half, fwd, bwd) * sin

    def rope_wide(xw):
        return jnp.concatenate([rope128(xw[:, c * LANES:(c + 1) * LANES]) for c in range(xw.shape[1] // LANES)], axis=1)

    cq = _rms(p[:, A_QD:A_QD + Q_LORA], gqa_ref[...], RMS_EPS).astype(BF16)
    q = _dot(cq, wuq_ref[...])
    qn = q[:, :MLA_H * MLA_NOPE].astype(BF16)
    qr = rope_wide(q[:, MLA_H * MLA_NOPE:]).astype(BF16)
    q_lat = _dot(qn, wn_ref[...])
    q_pe = _dot(qr, we_ref[...])
    for h in range(MLA_H):
        cols = slice(h * KV_LORA, (h + 1) * KV_LORA)
        qe = jnp.concatenate([q_lat[:, cols], q_pe[:, cols]], axis=1)
        qm_ref[0, h] = (qe * (MLA_SCALE * LOG2E)).astype(BF16)

    ckv = _rms(p[:, A_KV:A_KV + KV_LORA], gkva_ref[...], RMS_EPS)
    ckv_ref[0] = ckv
    kseg = rope128(p[:, A_KR:A_KR + LANES])
    kpe_ref[0] = kseg[:, :MLA_ROPE]
    kvm_ref[0] = jnp.concatenate([ckv, kseg], axis=1).astype(BF16)

    dq = rope_wide(p[:, A_DQ:A_DQ + DIFF_W])
    dqb_ref[0] = (dq * (DIFF_SCALE * LOG2E)).astype(BF16)
    dk = rope_wide(p[:, A_DK:A_DK + DIFF_W])
    dk_ref[0] = dk
    dkb_ref[0] = dk.astype(BF16)
    dv = p[:, A_DV:A_DV + DIFF_W]
    dv_ref[0] = dv
    dvb_ref[0] = dv.astype(BF16)


def _proj(x, cos, sin, w):
    b, l, d = x.shape
    tm = min(512, l)
    assert l % tm == 0
    grid = (l // tm, b)

    def full(a):
        return pl.BlockSpec(a.shape, lambda i, j, n=a.ndim: (0,) * n)

    def tok(width):
        return pl.BlockSpec((1, tm, width), lambda i, j: (j, i, 0))

    weights = (w['g_attn'], w['w_a'], w['g_q_a'], w['w_uq'], w['w_n'], w['w_e'], w['g_kv_a'])
    out_shape = (
        jax.ShapeDtypeStruct((b, l, KV_LORA), F32),
        jax.ShapeDtypeStruct((b, l, MLA_ROPE), F32),
        jax.ShapeDtypeStruct((b, l, DIFF_W), F32),
        jax.ShapeDtypeStruct((b, l, DIFF_W), F32),
        jax.ShapeDtypeStruct((b, MLA_H, l, QK_PAD), BF16),
        jax.ShapeDtypeStruct((b, l, QK_PAD), BF16),
        jax.ShapeDtypeStruct((b, l, DIFF_W), BF16),
        jax.ShapeDtypeStruct((b, l, DIFF_W), BF16),
        jax.ShapeDtypeStruct((b, l, DIFF_W), BF16),
    )
    out_specs = (tok(KV_LORA), tok(MLA_ROPE), tok(DIFF_W), tok(DIFF_W),
                 pl.BlockSpec((1, MLA_H, tm, QK_PAD), lambda i, j: (j, 0, i, 0)),
                 tok(QK_PAD), tok(DIFF_W), tok(DIFF_W), tok(DIFF_W))
    tab = pl.BlockSpec((tm, LANES), lambda i, j: (i, 0))
    return pl.pallas_call(
        _proj_kernel,
        grid=grid,
        in_specs=[tok(d), tab, tab] + [full(a) for a in weights],
        out_specs=out_specs,
        out_shape=out_shape,
        compiler_params=pltpu.CompilerParams(dimension_semantics=("arbitrary", "arbitrary"),
                                             vmem_limit_bytes=VMEM_LIMIT),
        name="proj",
    )(x, cos, sin, *weights)


def _attn_kernel(qm_ref, dq_ref, kvm_ref, dk_ref, dv_ref, lam_ref, olat_ref, odiff_ref,
                 qd_ref, m_ref, l_ref, acc_ref, *, tq, tk, q_pos0, n_keys, lam_init):
    i = pl.program_id(1)

    lane = lax.broadcasted_iota(jnp.int32, (tq, LANES), 1)
    seg = lax.shift_right_logical(lane, 5)
    for p in range(N_PAIR):
        qp = dq_ref[0, :, p * LANES:(p + 1) * LANES].astype(F32)
        for j in range(MAPS_PER_PAIR):
            blk = MAPS_PER_PAIR * p + j
            qd_ref[blk * tq:(blk + 1) * tq] = jnp.where(seg == j, qp, 0.0).astype(BF16)

    m_ref[...] = jnp.full(m_ref.shape, NEG_INF, F32)
    l_ref[...] = jnp.zeros(l_ref.shape, F32)
    acc_ref[...] = jnp.zeros(acc_ref.shape, F32)

    q_lo = q_pos0 + i * tq
    q_hi = q_lo + tq - 1
    vis_all = jnp.minimum((q_lo // CHUNK + 1) * CHUNK, n_keys)
    vis_any = jnp.minimum((q_hi // CHUNK + 1) * CHUNK, n_keys)
    n_full = vis_all // tk
    n_tot = (vis_any + tk - 1) // tk

    merge = max(1, MERGE_ROWS // tq)
    g_mla, g_diff = min(MLA_H, merge), min(MAPS_PER_PAIR, merge)
    work = [(b, g_mla) for b in range(0, MLA_H, g_mla)]
    work += [(MLA_H + b, g_diff) for b in range(0, MAPS_PER_PAIR * N_PAIR, g_diff)]

    def step(start, width, masked):
        if masked:
            k_pos = start + lax.broadcasted_iota(jnp.int32, (1, width), 1)
            k_chunk = jnp.where(k_pos < n_keys, lax.shift_right_logical(k_pos, CHUNK_SHIFT), jnp.int32(2 ** 30))
        for blk, g in work:
            n_rows = g * tq
            rows = slice(blk * tq, blk * tq + n_rows)
            if blk < MLA_H:
                q = qm_ref[0, blk] if g == 1 else qm_ref[0, blk:blk + g].reshape(n_rows, QK_PAD)
                k = kvm_ref[0, pl.ds(start, width), :]
                v = kvm_ref[0, pl.ds(start, width), 0:KV_LORA]
            else:
                p = (blk - MLA_H) // MAPS_PER_PAIR
                q = qd_ref[(blk - MLA_H) * tq:(blk - MLA_H) * tq + n_rows]
                k = dk_ref[0, pl.ds(start, width), p * LANES:(p + 1) * LANES]
                v = dv_ref[0, pl.ds(start, width), p * LANES:(p + 1) * LANES]
            s = _dot_nt(q, k)
            if masked:
                q_row = lax.broadcasted_iota(jnp.int32, (n_rows, 1), 0) & (tq - 1)
                q_chunk = lax.shift_right_logical(q_lo + q_row, CHUNK_SHIFT)
                s = jnp.where(k_chunk <= q_chunk, s, NEG_INF)
            m_old = m_ref[rows]
            m_new = jnp.maximum(m_old, jnp.max(s, axis=1, keepdims=True))
            alpha = jnp.exp2(m_old - m_new)
            ps = [jnp.exp2(s[:, c * LANES:(c + 1) * LANES] - m_new) for c in range(width // LANES)]
            l_ref[rows] = alpha * l_ref[rows] + sum(ps)
            acc_ref[rows] = alpha * acc_ref[rows] + _dot(jnp.concatenate(ps, axis=1).astype(BF16), v)
            m_ref[rows] = m_new

    def full_pair(j, c):
        step(pl.multiple_of(2 * j * tk, tk), tk, False)
        step(pl.multiple_of((2 * j + 1) * tk, tk), tk, False)
        return c

    def full_step(kt, c):
        step(pl.multiple_of(kt * tk, tk), tk, False)
        return c

    def masked_step(kt, c):
        step(pl.multiple_of(kt * tk, tk), tk, True)
        return c

    n_pairs = n_full // 2
    lax.fori_loop(0, n_pairs, full_pair, 0)
    lax.fori_loop(2 * n_pairs, n_full, full_step, 0)
    lax.fori_loop(n_full, n_tot, masked_step, 0)

    def normalised(blk):
        rows = slice(blk * tq, (blk + 1) * tq)
        return acc_ref[rows] / jnp.sum(l_ref[rows], axis=1, keepdims=True)

    for h in range(MLA_H):
        olat_ref[0, :, h * LANES:(h + 1) * LANES] = normalised(h).astype(BF16)

    lam = (jnp.exp(jnp.sum(lam_ref[0:1] * lam_ref[1:2], axis=1, keepdims=True))
           - jnp.exp(jnp.sum(lam_ref[2:3] * lam_ref[3:4], axis=1, keepdims=True)) + lam_init)
    for p in range(N_PAIR):
        n = [normalised(MLA_H + MAPS_PER_PAIR * p + j) for j in range(MAPS_PER_PAIR)]
        head0 = n[0] - lam * n[1]
        head1 = n[2] - lam * n[3]
        odiff_ref[0, :, p * LANES:(p + 1) * LANES] = jnp.where(lane < 2 * DIFF_D, head0, head1)


def _attention(qm, dqb, kvm, dkb, dvb, lam_rows, *, q_pos0, n_keys, lam_init):
    b, _, l, _ = qm.shape
    lk = kvm.shape[1]
    tq = min(256, l)
    tk = 256
    assert l % tq == 0 and lk % tk == 0 and tq & (tq - 1) == 0
    n_rows = (MLA_H + MAPS_PER_PAIR * N_PAIR) * tq
    kern = functools.partial(_attn_kernel, tq=tq, tk=tk, q_pos0=q_pos0, n_keys=n_keys, lam_init=lam_init)
    per_batch = lambda width: pl.BlockSpec((1, lk, width), lambda j, i: (j, 0, 0))
    return pl.pallas_call(
        kern,
        grid=(b, l // tq),
        in_specs=[pl.BlockSpec((1, MLA_H, tq, QK_PAD), lambda j, i: (j, 0, i, 0)),
                  pl.BlockSpec((1, tq, DIFF_W), lambda j, i: (j, i, 0)),
                  per_batch(QK_PAD), per_batch(DIFF_W), per_batch(DIFF_W),
                  pl.BlockSpec(lam_rows.shape, lambda j, i: (0, 0))],
        out_specs=(pl.BlockSpec((1, tq, MLA_H * KV_LORA), lambda j, i: (j, i, 0)),
                   pl.BlockSpec((1, tq, DIFF_W), lambda j, i: (j, i, 0))),
        out_shape=(jax.ShapeDtypeStruct((b, l, MLA_H * KV_LORA), BF16),
                   jax.ShapeDtypeStruct((b, l, DIFF_W), F32)),
        scratch_shapes=[pltpu.VMEM((MAPS_PER_PAIR * N_PAIR * tq, LANES), BF16),
                        pltpu.VMEM((n_rows, LANES), F32),
                        pltpu.VMEM((n_rows, LANES), F32),
                        pltpu.VMEM((n_rows, LANES), F32)],
        compiler_params=pltpu.CompilerParams(dimension_semantics=("arbitrary", "arbitrary"),
                                             vmem_limit_bytes=VMEM_LIMIT),
        name="attn",
    )(qm, dqb, kvm, dkb, dvb, lam_rows)


def _post_kernel(x_ref, olat_ref, odiff_ref, cnt_in_ref, su_ref, gattn_ref, wg_ref, wuv_ref, womla_ref, bd_ref,
                 gsub_ref, wodiff_ref, wout_ref, gffn_ref, rwt_ref, rb_ref,
                 h_ref, tok_ref, topi_ref, rank_ref, gate_ref, cnt_ref, *, lam_init):
    tm = x_ref.shape[0]
    x = x_ref[...]
    xn = _rms(x, gattn_ref[...], RMS_EPS).astype(BF16)
    g = _dot(xn, wg_ref[...])
    y_mla = _dot(_dot(olat_ref[...], wuv_ref[...]).astype(BF16), womla_ref[...])

    od = odiff_ref[...]
    sq = od * od
    sq_hi = sq.astype(BF16)
    sq_lo = (sq - sq_hi.astype(F32)).astype(BF16)
    ms = _dot(sq_hi, bd_ref[...]) + _dot(sq_lo, bd_ref[...])
    on = od * lax.rsqrt(ms + SUBLN_EPS) * gsub_ref[...] * (1.0 - lam_init)
    y_diff = _dot(on.astype(BF16), wodiff_ref[...])

    merged = jax.nn.sigmoid(g[:, :D_MODEL]) * y_mla + jax.nn.sigmoid(g[:, D_MODEL:]) * y_diff
    h = x + _dot(merged.astype(BF16), wout_ref[...])
    h_ref[...] = h
    tok = _rms(h, gffn_ref[...], RMS_EPS)
    tok_ref[...] = tok

    logits = _dot_nt(rwt_ref[...], tok.astype(BF16)) + rb_ref[:, 0:1]
    e_iota = lax.broadcasted_iota(jnp.int32, (N_EXPERTS, tm), 0).astype(F32)
    work = logits
    vals, hots = [], []
    for k in range(TOP_K):
        mx = jnp.max(work, axis=0, keepdims=True)
        idx = jnp.min(jnp.where(work == mx, e_iota, float(N_EXPERTS)), axis=0, keepdims=True)
        hot = e_iota == idx
        work = jnp.where(hot, -jnp.inf, work)
        vals.append(mx)
        hots.append(hot)
        topi_ref[k:k + 1, :] = idx.astype(jnp.int32)
    exps = [jnp.exp(v - vals[0]) for v in vals]
    den = exps[0] + exps[1] + exps[2] + exps[3]
    for k in range(TOP_K):
        gate_ref[k:k + 1, :] = exps[k] / den

    @pl.when(pl.program_id(0) == 0)
    def _():
        cnt_ref[...] = cnt_in_ref[...]

    sel = sum(jnp.where(hot, 1.0, 0.0) for hot in hots)
    before = _dot(sel.astype(BF16), su_ref[...]) + cnt_ref[:, 0:1]
    for k in range(TOP_K):
        rank_ref[k:k + 1, :] = jnp.sum(jnp.where(hots[k], before, 0.0), axis=0, keepdims=True).astype(jnp.int32)
    cnt_ref[...] = cnt_ref[...] + jnp.sum(sel, axis=1, keepdims=True)


def _post(x2d, olat2d, odiff2d, cnt_in, w, *, lam_init):
    t, d = x2d.shape
    tm = min(512, t)
    assert t % tm == 0
    su = (lax.broadcasted_iota(jnp.int32, (tm, tm), 0) < lax.broadcasted_iota(jnp.int32, (tm, tm), 1)).astype(BF16)
    weights = (w['g_attn'], w['w_g'], w['w_uv_bd'], w['w_o_mla'], w['bd64'], w['g_subln'], w['w_o_diff'],
               w['w_out'], w['g_ffn'], w['rw_t'], w['rb'])

    def full(a):
        return pl.BlockSpec(a.shape, lambda i, n=a.ndim: (0,) * n)

    tokspec = lambda width: pl.BlockSpec((tm, width), lambda i: (i, 0))
    small = pl.BlockSpec((TOP_K, tm), lambda i: (0, i))
    kern = functools.partial(_post_kernel, lam_init=lam_init)
    return pl.pallas_call(
        kern,
        grid=(t // tm,),
        in_specs=[tokspec(d), tokspec(MLA_H * KV_LORA), tokspec(DIFF_W), full(cnt_in), full(su)]
                 + [full(a) for a in weights],
        out_specs=(tokspec(d), tokspec(d), small, small, small, full(cnt_in)),
        out_shape=(jax.ShapeDtypeStruct((t, d), F32), jax.ShapeDtypeStruct((t, d), F32),
                   jax.ShapeDtypeStruct((TOP_K, t), jnp.int32), jax.ShapeDtypeStruct((TOP_K, t), jnp.int32),
                   jax.ShapeDtypeStruct((TOP_K, t), F32), jax.ShapeDtypeStruct(cnt_in.shape, F32)),
        compiler_params=pltpu.CompilerParams(dimension_semantics=("arbitrary",), vmem_limit_bytes=VMEM_LIMIT),
        name="post",
    )(x2d, olat2d, odiff2d, cnt_in, su, *weights)


def _dispatch_kernel(pos_ref, tok_ref, *rest):
    xs_ref, sems = rest[-2], rest[-1]
    tm = tok_ref.shape[0]

    for t in range(tm):
        for k in range(TOP_K):
            q = k % N_DMA_PRIORITIES
            pltpu.make_async_copy(tok_ref.at[pl.ds(t, 1)], xs_ref.at[pl.ds(pos_ref[k, t], 1)],
                                  sems.at[q]).start(priority=q)
    for k in range(TOP_K):
        pltpu.make_async_copy(tok_ref, xs_ref.at[pl.ds(0, tm)], sems.at[k % N_DMA_PRIORITIES]).wait()


def _dispatch(pos, tok, xs, n_slots):
    t, d = tok.shape
    tm = min(256, t)
    assert t % tm == 0
    in_specs = [pl.BlockSpec((TOP_K, tm), lambda i: (0, i), memory_space=pltpu.SMEM),
                pl.BlockSpec((tm, d), lambda i: (i, 0))]
    args = [pos, tok]
    aliases = {}
    if xs is not None:
        in_specs.append(pl.BlockSpec(memory_space=pl.ANY))
        args.append(xs)
        aliases = {2: 0}
    return pl.pallas_call(
        _dispatch_kernel,
        grid=(t // tm,),
        in_specs=in_specs,
        out_specs=pl.BlockSpec(memory_space=pl.ANY),
        out_shape=jax.ShapeDtypeStruct((n_slots, d), F32),
        scratch_shapes=[pltpu.SemaphoreType.DMA((N_DMA_PRIORITIES,))],
        input_output_aliases=aliases,
        compiler_params=pltpu.CompilerParams(dimension_semantics=("arbitrary",), has_side_effects=True),
        name="dispatch",
    )(*args)


def _expert_kernel(be_ref, nv_ref, xs_ref, wgu_ref, bgu_ref, wd_ref, bd_ref, ys_ref, wgu_bf_ref, wd_bf_ref):
    i = pl.program_id(0)
    r = xs_ref.shape[0]
    nv = nv_ref[i]

    @pl.when(nv > 0)
    def _():
        @pl.when(jnp.logical_or(i == 0, be_ref[i] != be_ref[jnp.maximum(i - 1, 0)]))
        def _():
            wgu_bf_ref[...] = wgu_ref[0].astype(BF16)
            wd_bf_ref[...] = wd_ref[0].astype(BF16)

        rows = lax.broadcasted_iota(jnp.int32, (r, 1), 0)
        x = jnp.where(rows < nv, xs_ref[...], 0.0).astype(BF16)
        h = _dot(x, wgu_bf_ref[...]) + bgu_ref[0]
        gate = jnp.minimum(h[:, :D_FF], SWIGLU_LIMIT)
        up = jnp.clip(h[:, D_FF:], -SWIGLU_LIMIT, SWIGLU_LIMIT)
        act = gate * jax.nn.sigmoid(SWIGLU_ALPHA * gate) * (up + 1.0)
        ys_ref[...] = _dot(act.astype(BF16), wd_bf_ref[...]) + bd_ref[0]


def _experts(block_e, n_valid, xs, wgu, bgu, wd, bd, rows_per_block):
    n_slots, d = xs.shape
    r = rows_per_block
    n_blocks = n_slots // r
    grid_spec = pltpu.PrefetchScalarGridSpec(
        num_scalar_prefetch=2,
        grid=(n_blocks,),
        in_specs=[pl.BlockSpec((r, d), lambda i, be, nv: (i, 0)),
                  pl.BlockSpec((1, d, 2 * D_FF), lambda i, be, nv: (be[i], 0, 0)),
                  pl.BlockSpec((1, 1, 2 * D_FF), lambda i, be, nv: (be[i], 0, 0)),
                  pl.BlockSpec((1, D_FF, d), lambda i, be, nv: (be[i], 0, 0)),
                  pl.BlockSpec((1, 1, d), lambda i, be, nv: (be[i], 0, 0))],
        out_specs=pl.BlockSpec((r, d), lambda i, be, nv: (i, 0)),
        scratch_shapes=[pltpu.VMEM((d, 2 * D_FF), BF16), pltpu.VMEM((D_FF, d), BF16)],
    )
    return pl.pallas_call(
        _expert_kernel,
        grid_spec=grid_spec,
        out_shape=jax.ShapeDtypeStruct((n_slots, d), F32),
        compiler_params=pltpu.CompilerParams(dimension_semantics=("arbitrary",), vmem_limit_bytes=VMEM_LIMIT),
        name="experts",
    )(block_e, n_valid, xs, wgu, bgu, wd, bd)


def _combine_kernel(pos_ref, pos_next_ref, gate_ref, h_ref, gfin_ref, ys_ref, out_ref, buf_ref, sems):
    tm = h_ref.shape[0]
    i = pl.program_id(0)
    last = pl.num_programs(0) - 1
    slot = i % 2

    def gather(idx_ref, s):
        for t in range(tm):
            for k in range(TOP_K):
                q = k % N_DMA_PRIORITIES
                pltpu.make_async_copy(ys_ref.at[pl.ds(idx_ref[k, t], 1)], buf_ref.at[s, k, pl.ds(t, 1)],
                                      sems.at[s, q]).start(priority=q)

    def wait_tile(s):
        for k in range(TOP_K):
            pltpu.make_async_copy(ys_ref.at[pl.ds(0, tm)], buf_ref.at[s, k], sems.at[s, k % N_DMA_PRIORITIES]).wait()

    @pl.when(i == 0)
    def _():
        gather(pos_ref, 0)

    def tile(s):
        wait_tile(s)
        gather(pos_next_ref, 1 - s)
        eye = lax.broadcasted_iota(jnp.int32, (tm, tm), 0) == lax.broadcasted_iota(jnp.int32, (tm, tm), 1)
        acc = h_ref[...]
        for k in range(TOP_K):
            g_col = jnp.sum(jnp.where(eye, gate_ref[k:k + 1, :], 0.0), axis=1, keepdims=True)
            acc = acc + g_col * buf_ref[s, k]
        out_ref[...] = _rms(acc, gfin_ref[...], RMS_EPS)

    for s in range(2):
        pl.when(slot == s)(functools.partial(tile, s))

    @pl.when(i == last)
    def _():
        wait_tile(1 - slot)


def _combine(pos, gate, h, g_final, ys):
    t, d = h.shape
    tm = min(256, t)
    assert t % tm == 0
    n_tiles = t // tm
    return pl.pallas_call(
        _combine_kernel,
        grid=(n_tiles,),
        in_specs=[pl.BlockSpec((TOP_K, tm), lambda i: (0, i), memory_space=pltpu.SMEM),
                  pl.BlockSpec((TOP_K, tm), lambda i: (0, jnp.minimum(i + 1, n_tiles - 1)), memory_space=pltpu.SMEM),
                  pl.BlockSpec((TOP_K, tm), lambda i: (0, i)),
                  pl.BlockSpec((tm, d), lambda i: (i, 0)),
                  pl.BlockSpec(g_final.shape, lambda i: (0, 0)),
                  pl.BlockSpec(memory_space=pl.ANY)],
        out_specs=pl.BlockSpec((tm, d), lambda i: (i, 0)),
        out_shape=jax.ShapeDtypeStruct((t, d), F32),
        scratch_shapes=[pltpu.VMEM((2, TOP_K, tm, d), F32), pltpu.SemaphoreType.DMA((2, N_DMA_PRIORITIES))],
        compiler_params=pltpu.CompilerParams(dimension_semantics=("arbitrary",), vmem_limit_bytes=VMEM_LIMIT),
        name="combine",
    )(pos, pos, gate, h, g_final, ys)


def _rope_tables(pos):
    half = DIFF_D // 2
    inv = ROPE_THETA ** (-jnp.arange(half, dtype=F32) / half)
    ang = pos.astype(F32)[:, None] * inv[None, :]
    cos, sin = jnp.cos(ang), jnp.sin(ang)
    reps = LANES // DIFF_D
    return (jnp.tile(jnp.concatenate([cos, cos], axis=1), (1, reps)),
            jnp.tile(jnp.concatenate([-sin, sin], axis=1), (1, reps)))


def _prep_weights(w_in, g_attn_norm, g_q_a, w_uq, g_kv_a, w_uk, w_uv, w_o_mla, g_subln, w_o_diff, w_out,
                  g_ffn_norm, router_w, router_b):
    o_q, o_kv, o_kr, o_dq, o_dk, o_dv, o_g = 0, 256, 384, 416, 928, 1440, 1952
    w_a = jnp.concatenate([w_in[:, o_q:o_kv], w_in[:, o_kv:o_kr], w_in[:, o_dq:o_dk], w_in[:, o_dk:o_dv],
                           w_in[:, o_dv:o_g], w_in[:, o_kr:o_dq],
                           jnp.zeros((D_MODEL, LANES - MLA_ROPE), w_in.dtype)], axis=1).astype(BF16)
    w_g = w_in[:, o_g:].astype(BF16)
    uq = w_uq.reshape(Q_LORA, MLA_H, MLA_QK)
    w_uq_p = jnp.concatenate([uq[:, :, :MLA_NOPE].reshape(Q_LORA, MLA_H * MLA_NOPE),
                              uq[:, :, MLA_NOPE:].reshape(Q_LORA, MLA_H * MLA_ROPE)], axis=1).astype(BF16)
    same_head = jnp.eye(MLA_H, dtype=bool)[:, None, :, None]
    ukt = jnp.transpose(w_uk, (1, 2, 0))
    w_n = jnp.where(same_head, ukt[:, :, None, :], 0.0).reshape(MLA_H * MLA_NOPE, MLA_H * KV_LORA).astype(BF16)
    place = jnp.eye(MLA_ROPE, KV_LORA, dtype=F32)
    w_e = jnp.where(same_head, place[None, :, None, :], 0.0).reshape(MLA_H * MLA_ROPE, MLA_H * KV_LORA).astype(BF16)
    uv = jnp.transpose(w_uv, (1, 0, 2))
    w_uv_bd = jnp.where(same_head, uv[:, :, None, :], 0.0).reshape(MLA_H * KV_LORA, MLA_H * MLA_V).astype(BF16)
    grp = jnp.arange(DIFF_W) // (2 * DIFF_D)
    bd64 = jnp.where(grp[:, None] == grp[None, :], 1.0 / (2 * DIFF_D), 0.0).astype(BF16)
    return {
        'g_attn': g_attn_norm.reshape(1, D_MODEL), 'w_a': w_a, 'w_g': w_g,
        'g_q_a': g_q_a.reshape(1, Q_LORA), 'w_uq': w_uq_p, 'w_n': w_n, 'w_e': w_e,
        'g_kv_a': g_kv_a.reshape(1, KV_LORA), 'w_uv_bd': w_uv_bd, 'w_o_mla': w_o_mla.astype(BF16),
        'bd64': bd64, 'g_subln': jnp.tile(g_subln, DIFF_H).reshape(1, DIFF_W),
        'w_o_diff': w_o_diff.astype(BF16), 'w_out': w_out.astype(BF16),
        'g_ffn': g_ffn_norm.reshape(1, D_MODEL), 'rw_t': router_w.T.astype(BF16),
        'rb': jnp.broadcast_to(router_b.reshape(N_EXPERTS, 1), (N_EXPERTS, LANES)),
    }


def _pad_keys(a, tk):
    lk = a.shape[1]
    pad = (-lk) % tk
    if pad == 0:
        return a
    return jnp.concatenate([a, jnp.zeros((a.shape[0], pad) + a.shape[2:], a.dtype)], axis=1)


def _slot_base(topi, pstart):
    e = jnp.arange(N_EXPERTS, dtype=jnp.int32)
    return jnp.sum(jnp.where(topi[..., None] == e, pstart, 0), axis=-1)


def kernel(x_prompt, x_sample, cache_mla_ckv, cache_mla_kpe, cache_diff_k, cache_diff_v, g_attn_norm, w_in, g_q_a, w_uq, g_kv_a, w_uk, w_uv, w_o_mla, lambda_q1, lambda_k1, lambda_q2, lambda_k2, g_subln, w_o_diff, w_out, g_ffn_norm, router_w, router_b, w_gate_up, b_gate_up, w_down, b_down, g_final):
    depth = w_in.shape[0]
    assert depth == 1
    li = 0
    lam_init = 0.8 - 0.6 * math.exp(-0.3 * li)
    bp, lp, d = x_prompt.shape
    bs, ls, _ = x_sample.shape
    past = cache_mla_ckv.shape[2]
    tk = 256

    w = _prep_weights(w_in[li], g_attn_norm[li], g_q_a[li], w_uq[li], g_kv_a[li], w_uk[li], w_uv[li], w_o_mla[li],
                      g_subln[li], w_o_diff[li], w_out[li], g_ffn_norm[li], router_w[li], router_b[li])
    lam_rows = jnp.stack([lambda_q1[li], lambda_k1[li], lambda_q2[li], lambda_k2[li]]).astype(F32)

    cos_p, sin_p = _rope_tables(jnp.arange(lp, dtype=jnp.int32))
    ckv_p, kpe_p, dk_p, dv_p, qm_p, kvm_p, dqb_p, dkb_p, dvb_p = _proj(x_prompt, cos_p, sin_p, w)
    olat_p, odiff_p = _attention(qm_p, dqb_p, kvm_p, dkb_p, dvb_p, lam_rows, q_pos0=0, n_keys=lp, lam_init=lam_init)

    cos_s, sin_s = _rope_tables(jnp.arange(past, past + ls, dtype=jnp.int32))
    ckv_s, kpe_s, dk_s, dv_s, qm_s, kvm_s, dqb_s, dkb_s, dvb_s = _proj(x_sample, cos_s, sin_s, w)
    kvm_past = jnp.concatenate([cache_mla_ckv[li], cache_mla_kpe[li],
                                jnp.zeros((bs, past, QK_PAD - KV_LORA - MLA_ROPE), F32)], axis=2).astype(BF16)
    kvm_all = _pad_keys(jnp.concatenate([kvm_past, kvm_s], axis=1), tk)
    dkb_all = _pad_keys(jnp.concatenate([cache_diff_k[li].reshape(bs, past, DIFF_W).astype(BF16), dkb_s], axis=1), tk)
    dvb_all = _pad_keys(jnp.concatenate([cache_diff_v[li].reshape(bs, past, DIFF_W).astype(BF16), dvb_s], axis=1), tk)
    olat_s, odiff_s = _attention(qm_s, dqb_s, kvm_all, dkb_all, dvb_all, lam_rows,
                                 q_pos0=past, n_keys=past + ls, lam_init=lam_init)

    tp, ts = bp * lp, bs * ls
    cnt0 = jnp.zeros((N_EXPERTS, LANES), F32)
    h_p, tok_p, topi_p, rank_p, gate_p, cnt1 = _post(x_prompt.reshape(tp, d), olat_p.reshape(tp, -1),
                                                     odiff_p.reshape(tp, -1), cnt0, w, lam_init=lam_init)
    h_s, tok_s, topi_s, rank_s, gate_s, cnt2 = _post(x_sample.reshape(ts, d), olat_s.reshape(ts, -1),
                                                     odiff_s.reshape(ts, -1), cnt1, w, lam_init=lam_init)

    r = 512
    n_assign = (tp + ts) * TOP_K
    n_blocks = -(-n_assign // r) + N_EXPERTS
    n_slots = n_blocks * r
    counts = cnt2[:, 0].astype(jnp.int32)
    padded = (counts + r - 1) // r * r
    pend = jnp.cumsum(padded)
    pstart = pend - padded
    blk_start = jnp.arange(n_blocks, dtype=jnp.int32) * r
    block_e = jnp.minimum(jnp.sum((pend[None, :] <= blk_start[:, None]).astype(jnp.int32), axis=1), N_EXPERTS - 1)
    seg_end = _slot_base(block_e, pstart + counts)
    n_valid = jnp.where(blk_start < pend[-1], jnp.clip(seg_end - blk_start, 0, r), 0).astype(jnp.int32)
    pos_p = rank_p + _slot_base(topi_p, pstart)
    pos_s = rank_s + _slot_base(topi_s, pstart)

    xs = _dispatch(pos_p, tok_p, None, n_slots)
    xs = _dispatch(pos_s, tok_s, xs, n_slots)
    ys = _experts(block_e, n_valid, xs, w_gate_up[li], b_gate_up[li].reshape(N_EXPERTS, 1, -1),
                  w_down[li], b_down[li].reshape(N_EXPERTS, 1, -1), r)
    gfin = g_final.reshape(1, d)
    y_p = _combine(pos_p, gate_p, h_p, gfin, ys)
    y_s = _combine(pos_s, gate_s, h_s, gfin, ys)

    def stack(a, shape):
        return a.reshape((1,) + shape)

    return (y_p.reshape(bp, lp, d), y_s.reshape(bs, ls, d),
            stack(ckv_p, (bp, lp, KV_LORA)), stack(kpe_p, (bp, lp, MLA_ROPE)),
            stack(dk_p, (bp, lp, DIFF_H, 2, DIFF_D)), stack(dv_p, (bp, lp, DIFF_H, 2 * DIFF_D)),
            stack(ckv_s, (bs, ls, KV_LORA)), stack(kpe_s, (bs, ls, MLA_ROPE)),
            stack(dk_s, (bs, ls, DIFF_H, 2, DIFF_D)), stack(dv_s, (bs, ls, DIFF_H, 2 * DIFF_D)))
```

```python
import functools
import math

import jax
import jax.numpy as jnp
from jax import lax
from jax.experimental import pallas as pl
from jax.experimental.pallas import tpu as pltpu

F32 = jnp.float32
BF16 = jnp.bfloat16

D_MODEL = 1024
CHUNK = 64
CHUNK_SHIFT = 6
ROPE_THETA = 10000.0
RMS_EPS = 1e-6
SUBLN_EPS = 1e-5
NEG_INF = -1e30

MLA_H = 8
Q_LORA = 256
KV_LORA = 128
MLA_NOPE = 64
MLA_ROPE = 32
MLA_QK = MLA_NOPE + MLA_ROPE
MLA_V = 64
MLA_SCALE = MLA_QK ** -0.5

DIFF_H = 8
DIFF_D = 32
DIFF_W = DIFF_H * 2 * DIFF_D
DIFF_SCALE = DIFF_D ** -0.5
N_PAIR = DIFF_W // 128
MAPS_PER_PAIR = 4

N_EXPERTS = 32
TOP_K = 4
D_FF = D_MODEL
SWIGLU_ALPHA = 1.702
SWIGLU_LIMIT = 7.0

LANES = 128
QK_PAD = 256
LOG2E = math.log2(math.e)
VMEM_LIMIT = 56 * 1024 * 1024
N_DMA_PRIORITIES = 2
MERGE_ROWS = 256

A_QD = 0
A_KV = A_QD + Q_LORA
A_DQ = A_KV + KV_LORA
A_DK = A_DQ + DIFF_W
A_DV = A_DK + DIFF_W
A_KR = A_DV + DIFF_W
A_COLS = A_KR + LANES


def _rms(x, g, eps):
    return x * lax.rsqrt(jnp.mean(x * x, axis=-1, keepdims=True) + eps) * g


def _dot(a, b):
    return jnp.dot(a, b, preferred_element_type=F32)


def _dot_nt(a, b):
    return lax.dot_general(a, b, (((1,), (1,)), ((), ())), preferred_element_type=F32)


def _proj_kernel(x_ref, cos_ref, sin_ref, gattn_ref, wa_ref, gqa_ref, wuq_ref, wn_ref, we_ref, gkva_ref,
                 ckv_ref, kpe_ref, dk_ref, dv_ref, qm_ref, kvm_ref, dqb_ref, dkb_ref, dvb_ref):
    tm = x_ref.shape[1]
    x = x_ref[0]
    xn = _rms(x, gattn_ref[...], RMS_EPS).astype(BF16)
    p = _dot(xn, wa_ref[...])

    cos = cos_ref[...]
    sin = sin_ref[...]
    lane = lax.broadcasted_iota(jnp.int32, (tm, LANES), 1)
    first_half = (lane & (DIFF_D - 1)) < (DIFF_D // 2)

    def rope128(xc):
        fwd = pltpu.roll(xc, LANES - DIFF_D // 2, 1)
        bwd = pltpu.roll(xc, DIFF_D // 2, 1)
        return xc * cos + jnp.where(first_half, fwd, bwd) * sin

    def rope_wide(xw):
        return jnp.concatenate([rope128(xw[:, c * LANES:(c + 1) * LANES]) for c in range(xw.shape[1] // LANES)], axis=1)

    cq = _rms(p[:, A_QD:A_QD + Q_LORA], gqa_ref[...], RMS_EPS).astype(BF16)
    q = _dot(cq, wuq_ref[...])
    qn = q[:, :MLA_H * MLA_NOPE].astype(BF16)
    qr = rope_wide(q[:, MLA_H * MLA_NOPE:]).astype(BF16)
    q_lat = _dot(qn, wn_ref[...])
    q_pe = _dot(qr, we_ref[...])
    for h in range(MLA_H):
        cols = slice(h * KV_LORA, (h + 1) * KV_LORA)
        qe = jnp.concatenate([q_lat[:, cols], q_pe[:, cols]], axis=1)
        qm_ref[0, h] = (qe * (MLA_SCALE * LOG2E)).astype(BF16)

    ckv = _rms(p[:, A_KV:A_KV + KV_LORA], gkva_ref[...], RMS_EPS)
    ckv_ref[0] = ckv
    kseg = rope128(p[:, A_KR:A_KR + LANES])
    kpe_ref[0] = kseg[:, :MLA_ROPE]
    kvm_ref[0] = jnp.concatenate([ckv, kseg], axis=1).astype(BF16)

    dq = rope_wide(p[:, A_DQ:A_DQ + DIFF_W])
    dqb_ref[0] = (dq * (DIFF_SCALE * LOG2E)).astype(BF16)
    dk = rope_wide(p[:, A_DK:A_DK + DIFF_W])
    dk_ref[0] = dk
    dkb_ref[0] = dk.astype(BF16)
    dv = p[:, A_DV:A_DV + DIFF_W]
    dv_ref[0] = dv
    dvb_ref[0] = dv.astype(BF16)


def _proj(x, cos, sin, w):
    b, l, d = x.shape
    tm = min(512, l)
    assert l % tm == 0
    grid = (l // tm, b)

    def full(a):
        return pl.BlockSpec(a.shape, lambda i, j, n=a.ndim: (0,) * n)

    def tok(width):
        return pl.BlockSpec((1, tm, width), lambda i, j: (j, i, 0))

    weights = (w['g_attn'], w['w_a'], w['g_q_a'], w['w_uq'], w['w_n'], w['w_e'], w['g_kv_a'])
    out_shape = (
        jax.ShapeDtypeStruct((b, l, KV_LORA), F32),
        jax.ShapeDtypeStruct((b, l, MLA_ROPE), F32),
        jax.ShapeDtypeStruct((b, l, DIFF_W), F32),
        jax.ShapeDtypeStruct((b, l, DIFF_W), F32),
        jax.ShapeDtypeStruct((b, MLA_H, l, QK_PAD), BF16),
        jax.ShapeDtypeStruct((b, l, QK_PAD), BF16),
        jax.ShapeDtypeStruct((b, l, DIFF_W), BF16),
        jax.ShapeDtypeStruct((b, l, DIFF_W), BF16),
        jax.ShapeDtypeStruct((b, l, DIFF_W), BF16),
    )
    out_specs = (tok(KV_LORA), tok(MLA_ROPE), tok(DIFF_W), tok(DIFF_W),
                 pl.BlockSpec((1, MLA_H, tm, QK_PAD), lambda i, j: (j, 0, i, 0)),
                 tok(QK_PAD), tok(DIFF_W), tok(DIFF_W), tok(DIFF_W))
    tab = pl.BlockSpec((tm, LANES), lambda i, j: (i, 0))
    return pl.pallas_call(
        _proj_kernel,
        grid=grid,
        in_specs=[tok(d), tab, tab] + [full(a) for a in weights],
        out_specs=out_specs,
        out_shape=out_shape,
        compiler_params=pltpu.CompilerParams(dimension_semantics=("arbitrary", "arbitrary"),
                                             vmem_limit_bytes=VMEM_LIMIT),
        name="proj",
    )(x, cos, sin, *weights)


def _attn_kernel(qm_ref, dq_ref, kvm_ref, dk_ref, dv_ref, lam_ref, olat_ref, odiff_ref,
                 qd_ref, m_ref, l_ref, acc_ref, *, tq, tk, q_pos0, n_keys, lam_init):
    i = pl.program_id(1)

    lane = lax.broadcasted_iota(jnp.int32, (tq, LANES), 1)
    seg = lax.shift_right_logical(lane, 5)
    for p in range(N_PAIR):
        qp = dq_ref[0, :, p * LANES:(p + 1) * LANES].astype(F32)
        for j in range(MAPS_PER_PAIR):
            blk = MAPS_PER_PAIR * p + j
            qd_ref[blk * tq:(blk + 1) * tq] = jnp.where(seg == j, qp, 0.0).astype(BF16)

    q_lo = q_pos0 + i * tq
    q_hi = q_lo + tq - 1
    vis_all = jnp.minimum((q_lo // CHUNK + 1) * CHUNK, n_keys)
    vis_any = jnp.minimum((q_hi // CHUNK + 1) * CHUNK, n_keys)
    n_full = vis_all // tk
    n_tot = (vis_any + tk - 1) // tk

    merge = max(1, MERGE_ROWS // tq)
    g_mla, g_diff = min(MLA_H, merge), min(MAPS_PER_PAIR, merge)
    work = [(b, g_mla) for b in range(0, MLA_H, g_mla)]
    work += [(MLA_H + b, g_diff) for b in range(0, MAPS_PER_PAIR * N_PAIR, g_diff)]

    def step(start, width, masked, first=False):
        if masked:
            k_pos = start + lax.broadcasted_iota(jnp.int32, (1, width), 1)
            k_chunk = jnp.where(k_pos < n_keys, lax.shift_right_logical(k_pos, CHUNK_SHIFT), jnp.int32(2 ** 30))
        for blk, g in work:
            n_rows = g * tq
            rows = slice(blk * tq, blk * tq + n_rows)
            if blk < MLA_H:
                q = qm_ref[0, blk] if g == 1 else qm_ref[0, blk:blk + g].reshape(n_rows, QK_PAD)
                k = kvm_ref[0, pl.ds(start, width), :]
                v = kvm_ref[0, pl.ds(start, width), 0:KV_LORA]
            else:
                p = (blk - MLA_H) // MAPS_PER_PAIR
                q = qd_ref[(blk - MLA_H) * tq:(blk - MLA_H) * tq + n_rows]
                k = dk_ref[0, pl.ds(start, width), p * LANES:(p + 1) * LANES]
                v = dv_ref[0, pl.ds(start, width), p * LANES:(p + 1) * LANES]
            s = _dot_nt(q, k)
            if masked:
                q_row = lax.broadcasted_iota(jnp.int32, (n_rows, 1), 0) & (tq - 1)
                q_chunk = lax.shift_right_logical(q_lo + q_row, CHUNK_SHIFT)
                s = jnp.where(k_chunk <= q_chunk, s, NEG_INF)
            s_max = jnp.max(s, axis=1, keepdims=True)
            if first:
                m_new = jnp.broadcast_to(s_max, (n_rows, LANES))
            else:
                m_old = m_ref[rows]
                m_new = jnp.maximum(m_old, s_max)
                alpha = jnp.exp2(m_old - m_new)
            ps = [jnp.exp2(s[:, c * LANES:(c + 1) * LANES] - m_new) for c in range(width // LANES)]
            pv = _dot(jnp.concatenate(ps, axis=1).astype(BF16), v)
            if first:
                l_ref[rows] = sum(ps)
                acc_ref[rows] = pv
            else:
                l_ref[rows] = alpha * l_ref[rows] + sum(ps)
                acc_ref[rows] = alpha * acc_ref[rows] + pv
            m_ref[rows] = m_new

    def full_steps(n, base):
        def body(j, c):
            for u in range(n):
                step(pl.multiple_of((base + n * j + u) * tk, tk), tk, False)
            return c
        return body

    def masked_step(kt, c):
        step(pl.multiple_of(kt * tk, tk), tk, True)
        return c

    @pl.when(n_full > 0)
    def _():
        step(0, tk, False, first=True)

    @pl.when(n_full == 0)
    def _():
        step(0, tk, True, first=True)

    n_rest = jnp.maximum(n_full - 1, 0)
    c4 = n_rest // 4
    c2 = (n_rest - 4 * c4) // 2
    c1 = n_rest - 4 * c4 - 2 * c2
    lax.fori_loop(0, c4, full_steps(4, 1), 0)
    lax.fori_loop(0, c2, full_steps(2, 1 + 4 * c4), 0)
    lax.fori_loop(0, c1, full_steps(1, 1 + 4 * c4 + 2 * c2), 0)
    lax.fori_loop(jnp.maximum(n_full, 1), n_tot, masked_step, 0)

    def normalised(blk):
        rows = slice(blk * tq, (blk + 1) * tq)
        return acc_ref[rows] / jnp.sum(l_ref[rows], axis=1, keepdims=True)

    for h in range(MLA_H):
        olat_ref[0, :, h * LANES:(h + 1) * LANES] = normalised(h).astype(BF16)

    lam = (jnp.exp(jnp.sum(lam_ref[0:1] * lam_ref[1:2], axis=1, keepdims=True))
           - jnp.exp(jnp.sum(lam_ref[2:3] * lam_ref[3:4], axis=1, keepdims=True)) + lam_init)
    for p in range(N_PAIR):
        n = [normalised(MLA_H + MAPS_PER_PAIR * p + j) for j in range(MAPS_PER_PAIR)]
        head0 = n[0] - lam * n[1]
        head1 = n[2] - lam * n[3]
        odiff_ref[0, :, p * LANES:(p + 1) * LANES] = jnp.where(lane < 2 * DIFF_D, head0, head1)


def _attention(qm, dqb, kvm, dkb, dvb, lam_rows, *, q_pos0, n_keys, lam_init):
    b, _, l, _ = qm.shape
    lk = kvm.shape[1]
    tq = min(256, l)
    tk = 256
    assert l % tq == 0 and lk % tk == 0 and tq & (tq - 1) == 0
    n_rows = (MLA_H + MAPS_PER_PAIR * N_PAIR) * tq
    kern = functools.partial(_attn_kernel, tq=tq, tk=tk, q_pos0=q_pos0, n_keys=n_keys, lam_init=lam_init)
    per_batch = lambda width: pl.BlockSpec((1, lk, width), lambda j, i: (j, 0, 0))
    return pl.pallas_call(
        kern,
        grid=(b, l // tq),
        in_specs=[pl.BlockSpec((1, MLA_H, tq, QK_PAD), lambda j, i: (j, 0, i, 0)),
                  pl.BlockSpec((1, tq, DIFF_W), lambda j, i: (j, i, 0)),
                  per_batch(QK_PAD), per_batch(DIFF_W), per_batch(DIFF_W),
                  pl.BlockSpec(lam_rows.shape, lambda j, i: (0, 0))],
        out_specs=(pl.BlockSpec((1, tq, MLA_H * KV_LORA), lambda j, i: (j, i, 0)),
                   pl.BlockSpec((1, tq, DIFF_W), lambda j, i: (j, i, 0))),
        out_shape=(jax.ShapeDtypeStruct((b, l, MLA_H * KV_LORA), BF16),
                   jax.ShapeDtypeStruct((b, l, DIFF_W), F32)),
        scratch_shapes=[pltpu.VMEM((MAPS_PER_PAIR * N_PAIR * tq, LANES), BF16),
                        pltpu.VMEM((n_rows, LANES), F32),
                        pltpu.VMEM((n_rows, LANES), F32),
                        pltpu.VMEM((n_rows, LANES), F32)],
        compiler_params=pltpu.CompilerParams(dimension_semantics=("arbitrary", "arbitrary"),
                                             vmem_limit_bytes=VMEM_LIMIT),
        name="attn",
    )(qm, dqb, kvm, dkb, dvb, lam_rows)


def _post_kernel(x_ref, olat_ref, odiff_ref, cnt_in_ref, su_ref, gattn_ref, wg_ref, wuv_ref, womla_ref, bd_ref,
                 gsub_ref, wodiff_ref, wout_ref, gffn_ref, rwt_ref, rb_ref,
                 h_ref, tok_ref, topi_ref, rank_ref, gate_ref, cnt_ref, *, lam_init):
    tm = x_ref.shape[0]
    x = x_ref[...]
    xn = _rms(x, gattn_ref[...], RMS_EPS).astype(BF16)
    g = _dot(xn, wg_ref[...])
    y_mla = _dot(_dot(olat_ref[...], wuv_ref[...]).astype(BF16), womla_ref[...])

    od = odiff_ref[...]
    sq = od * od
    sq_hi = sq.astype(BF16)
    sq_lo = (sq - sq_hi.astype(F32)).astype(BF16)
    ms = _dot(sq_hi, bd_ref[...]) + _dot(sq_lo, bd_ref[...])
    on = od * lax.rsqrt(ms + SUBLN_EPS) * gsub_ref[...] * (1.0 - lam_init)
    y_diff = _dot(on.astype(BF16), wodiff_ref[...])

    merged = jax.nn.sigmoid(g[:, :D_MODEL]) * y_mla + jax.nn.sigmoid(g[:, D_MODEL:]) * y_diff
    h = x + _dot(merged.astype(BF16), wout_ref[...])
    h_ref[...] = h
    tok = _rms(h, gffn_ref[...], RMS_EPS)
    tok_ref[...] = tok

    logits = _dot_nt(rwt_ref[...], tok.astype(BF16)) + rb_ref[:, 0:1]
    e_iota = lax.broadcasted_iota(jnp.int32, (N_EXPERTS, tm), 0).astype(F32)
    work = logits
    vals, hots = [], []
    for k in range(TOP_K):
        mx = jnp.max(work, axis=0, keepdims=True)
        idx = jnp.min(jnp.where(work == mx, e_iota, float(N_EXPERTS)), axis=0, keepdims=True)
        hot = e_iota == idx
        work = jnp.where(hot, -jnp.inf, work)
        vals.append(mx)
        hots.append(hot)
        topi_ref[k:k + 1, :] = idx.astype(jnp.int32)
    exps = [jnp.exp(v - vals[0]) for v in vals]
    den = exps[0] + exps[1] + exps[2] + exps[3]
    for k in range(TOP_K):
        gate_ref[k:k + 1, :] = exps[k] / den

    @pl.when(pl.program_id(0) == 0)
    def _():
        cnt_ref[...] = cnt_in_ref[...]

    sel = sum(jnp.where(hot, 1.0, 0.0) for hot in hots)
    before = _dot(sel.astype(BF16), su_ref[...]) + cnt_ref[:, 0:1]
    for k in range(TOP_K):
        rank_ref[k:k + 1, :] = jnp.sum(jnp.where(hots[k], before, 0.0), axis=0, keepdims=True).astype(jnp.int32)
    cnt_ref[...] = cnt_ref[...] + jnp.sum(sel, axis=1, keepdims=True)


def _post(x2d, olat2d, odiff2d, cnt_in, w, *, lam_init):
    t, d = x2d.shape
    tm = min(512, t)
    assert t % tm == 0
    su = (lax.broadcasted_iota(jnp.int32, (tm, tm), 0) < lax.broadcasted_iota(jnp.int32, (tm, tm), 1)).astype(BF16)
    weights = (w['g_attn'], w['w_g'], w['w_uv_bd'], w['w_o_mla'], w['bd64'], w['g_subln'], w['w_o_diff'],
               w['w_out'], w['g_ffn'], w['rw_t'], w['rb'])

    def full(a):
        return pl.BlockSpec(a.shape, lambda i, n=a.ndim: (0,) * n)

    tokspec = lambda width: pl.BlockSpec((tm, width), lambda i: (i, 0))
    small = pl.BlockSpec((TOP_K, tm), lambda i: (0, i))
    kern = functools.partial(_post_kernel, lam_init=lam_init)
    return pl.pallas_call(
        kern,
        grid=(t // tm,),
        in_specs=[tokspec(d), tokspec(MLA_H * KV_LORA), tokspec(DIFF_W), full(cnt_in), full(su)]
                 + [full(a) for a in weights],
        out_specs=(tokspec(d), tokspec(d), small, small, small, full(cnt_in)),
        out_shape=(jax.ShapeDtypeStruct((t, d), F32), jax.ShapeDtypeStruct((t, d), F32),
                   jax.ShapeDtypeStruct((TOP_K, t), jnp.int32), jax.ShapeDtypeStruct((TOP_K, t), jnp.int32),
                   jax.ShapeDtypeStruct((TOP_K, t), F32), jax.ShapeDtypeStruct(cnt_in.shape, F32)),
        compiler_params=pltpu.CompilerParams(dimension_semantics=("arbitrary",), vmem_limit_bytes=VMEM_LIMIT),
        name="post",
    )(x2d, olat2d, odiff2d, cnt_in, su, *weights)


def _dispatch_kernel(pos_ref, tok_ref, *rest):
    xs_ref, sems = rest[-2], rest[-1]
    tm = tok_ref.shape[0]

    for t in range(tm):
        for k in range(TOP_K):
            q = k % N_DMA_PRIORITIES
            pltpu.make_async_copy(tok_ref.at[pl.ds(t, 1)], xs_ref.at[pl.ds(pos_ref[k, t], 1)],
                                  sems.at[q]).start(priority=q)
    for k in range(TOP_K):
        pltpu.make_async_copy(tok_ref, xs_ref.at[pl.ds(0, tm)], sems.at[k % N_DMA_PRIORITIES]).wait()


def _dispatch(pos, tok, xs, n_slots):
    t, d = tok.shape
    tm = min(256, t)
    assert t % tm == 0
    in_specs = [pl.BlockSpec((TOP_K, tm), lambda i: (0, i), memory_space=pltpu.SMEM),
                pl.BlockSpec((tm, d), lambda i: (i, 0))]
    args = [pos, tok]
    aliases = {}
    if xs is not None:
        in_specs.append(pl.BlockSpec(memory_space=pl.ANY))
        args.append(xs)
        aliases = {2: 0}
    return pl.pallas_call(
        _dispatch_kernel,
        grid=(t // tm,),
        in_specs=in_specs,
        out_specs=pl.BlockSpec(memory_space=pl.ANY),
        out_shape=jax.ShapeDtypeStruct((n_slots, d), F32),
        scratch_shapes=[pltpu.SemaphoreType.DMA((N_DMA_PRIORITIES,))],
        input_output_aliases=aliases,
        compiler_params=pltpu.CompilerParams(dimension_semantics=("arbitrary",), has_side_effects=True),
        name="dispatch",
    )(*args)


def _expert_kernel(be_ref, nv_ref, xs_ref, wgu_ref, bgu_ref, wd_ref, bd_ref, ys_ref, wgu_bf_ref, wd_bf_ref):
    i = pl.program_id(0)
    r = xs_ref.shape[0]
    nv = nv_ref[i]

    @pl.when(nv > 0)
    def _():
        @pl.when(jnp.logical_or(i == 0, be_ref[i] != be_ref[jnp.maximum(i - 1, 0)]))
        def _():
            wgu_bf_ref[...] = wgu_ref[0].astype(BF16)
            wd_bf_ref[...] = wd_ref[0].astype(BF16)

        rows = lax.broadcasted_iota(jnp.int32, (r, 1), 0)
        x = jnp.where(rows < nv, xs_ref[...], 0.0).astype(BF16)
        h = _dot(x, wgu_bf_ref[...]) + bgu_ref[0]
        gate = jnp.minimum(h[:, :D_FF], SWIGLU_LIMIT)
        up = jnp.clip(h[:, D_FF:], -SWIGLU_LIMIT, SWIGLU_LIMIT)
        act = gate * jax.nn.sigmoid(SWIGLU_ALPHA * gate) * (up + 1.0)
        ys_ref[...] = _dot(act.astype(BF16), wd_bf_ref[...]) + bd_ref[0]


def _experts(block_e, n_valid, xs, wgu, bgu, wd, bd, rows_per_block):
    n_slots, d = xs.shape
    r = rows_per_block
    n_blocks = n_slots // r
    grid_spec = pltpu.PrefetchScalarGridSpec(
        num_scalar_prefetch=2,
        grid=(n_blocks,),
        in_specs=[pl.BlockSpec((r, d), lambda i, be, nv: (i, 0)),
                  pl.BlockSpec((1, d, 2 * D_FF), lambda i, be, nv: (be[i], 0, 0)),
                  pl.BlockSpec((1, 1, 2 * D_FF), lambda i, be, nv: (be[i], 0, 0)),
                  pl.BlockSpec((1, D_FF, d), lambda i, be, nv: (be[i], 0, 0)),
                  pl.BlockSpec((1, 1, d), lambda i, be, nv: (be[i], 0, 0))],
        out_specs=pl.BlockSpec((r, d), lambda i, be, nv: (i, 0)),
        scratch_shapes=[pltpu.VMEM((d, 2 * D_FF), BF16), pltpu.VMEM((D_FF, d), BF16)],
    )
    return pl.pallas_call(
        _expert_kernel,
        grid_spec=grid_spec,
        out_shape=jax.ShapeDtypeStruct((n_slots, d), F32),
        compiler_params=pltpu.CompilerParams(dimension_semantics=("arbitrary",), vmem_limit_bytes=VMEM_LIMIT),
        name="experts",
    )(block_e, n_valid, xs, wgu, bgu, wd, bd)


def _combine_kernel(pos_ref, pos_next_ref, gate_ref, h_ref, gfin_ref, ys_ref, out_ref, buf0_ref, buf1_ref, sems):
    tm = h_ref.shape[0]
    i = pl.program_id(0)
    last = pl.num_programs(0) - 1
    slot = i % 2
    bufs = (buf0_ref, buf1_ref)

    def gather(idx_ref, s):
        for t in range(tm):
            for k in range(TOP_K):
                q = k % N_DMA_PRIORITIES
                pltpu.make_async_copy(ys_ref.at[pl.ds(idx_ref[k, t], 1)], bufs[s].at[k, pl.ds(t, 1)],
                                      sems.at[s, q]).start(priority=q)

    def wait_tile(s):
        for k in range(TOP_K):
            pltpu.make_async_copy(ys_ref.at[pl.ds(0, tm)], bufs[s].at[k], sems.at[s, k % N_DMA_PRIORITIES]).wait()

    @pl.when(i == 0)
    def _():
        gather(pos_ref, 0)

    def tile(s):
        wait_tile(s)
        gather(pos_next_ref, 1 - s)
        eye = lax.broadcasted_iota(jnp.int32, (tm, tm), 0) == lax.broadcasted_iota(jnp.int32, (tm, tm), 1)
        acc = h_ref[...]
        for k in range(TOP_K):
            g_col = jnp.sum(jnp.where(eye, gate_ref[k:k + 1, :], 0.0), axis=1, keepdims=True)
            acc = acc + g_col * bufs[s][k]
        out_ref[...] = _rms(acc, gfin_ref[...], RMS_EPS)

        @pl.when(i == last)
        def _():
            wait_tile(1 - s)

    for s in range(2):
        pl.when(slot == s)(functools.partial(tile, s))


def _combine(pos, gate, h, g_final, ys):
    t, d = h.shape
    tm = min(256, t)
    assert t % tm == 0
    n_tiles = t // tm
    return pl.pallas_call(
        _combine_kernel,
        grid=(n_tiles,),
        in_specs=[pl.BlockSpec((TOP_K, tm), lambda i: (0, i), memory_space=pltpu.SMEM),
                  pl.BlockSpec((TOP_K, tm), lambda i: (0, jnp.minimum(i + 1, n_tiles - 1)), memory_space=pltpu.SMEM),
                  pl.BlockSpec((TOP_K, tm), lambda i: (0, i)),
                  pl.BlockSpec((tm, d), lambda i: (i, 0)),
                  pl.BlockSpec(g_final.shape, lambda i: (0, 0)),
                  pl.BlockSpec(memory_space=pl.ANY)],
        out_specs=pl.BlockSpec((tm, d), lambda i: (i, 0)),
        out_shape=jax.ShapeDtypeStruct((t, d), F32),
        scratch_shapes=[pltpu.VMEM((TOP_K, tm, d), F32), pltpu.VMEM((TOP_K, tm, d), F32),
                        pltpu.SemaphoreType.DMA((2, N_DMA_PRIORITIES))],
        compiler_params=pltpu.CompilerParams(dimension_semantics=("arbitrary",), vmem_limit_bytes=VMEM_LIMIT),
        name="combine",
    )(pos, pos, gate, h, g_final, ys)


def _rope_tables(pos):
    half = DIFF_D // 2
    inv = ROPE_THETA ** (-jnp.arange(half, dtype=F32) / half)
    ang = pos.astype(F32)[:, None] * inv[None, :]
    cos, sin = jnp.cos(ang), jnp.sin(ang)
    reps = LANES // DIFF_D
    return (jnp.tile(jnp.concatenate([cos, cos], axis=1), (1, reps)),
            jnp.tile(jnp.concatenate([-sin, sin], axis=1), (1, reps)))


def _prep_weights(w_in, g_attn_norm, g_q_a, w_uq, g_kv_a, w_uk, w_uv, w_o_mla, g_subln, w_o_diff, w_out,
                  g_ffn_norm, router_w, router_b):
    o_q, o_kv, o_kr, o_dq, o_dk, o_dv, o_g = 0, 256, 384, 416, 928, 1440, 1952
    w_a = jnp.concatenate([w_in[:, o_q:o_kv], w_in[:, o_kv:o_kr], w_in[:, o_dq:o_dk], w_in[:, o_dk:o_dv],
                           w_in[:, o_dv:o_g], w_in[:, o_kr:o_dq],
                           jnp.zeros((D_MODEL, LANES - MLA_ROPE), w_in.dtype)], axis=1).astype(BF16)
    w_g = w_in[:, o_g:].astype(BF16)
    uq = w_uq.reshape(Q_LORA, MLA_H, MLA_QK)
    w_uq_p = jnp.concatenate([uq[:, :, :MLA_NOPE].reshape(Q_LORA, MLA_H * MLA_NOPE),
                              uq[:, :, MLA_NOPE:].reshape(Q_LORA, MLA_H * MLA_ROPE)], axis=1).astype(BF16)
    same_head = jnp.eye(MLA_H, dtype=bool)[:, None, :, None]
    ukt = jnp.transpose(w_uk, (1, 2, 0))
    w_n = jnp.where(same_head, ukt[:, :, None, :], 0.0).reshape(MLA_H * MLA_NOPE, MLA_H * KV_LORA).astype(BF16)
    place = jnp.eye(MLA_ROPE, KV_LORA, dtype=F32)
    w_e = jnp.where(same_head, place[None, :, None, :], 0.0).reshape(MLA_H * MLA_ROPE, MLA_H * KV_LORA).astype(BF16)
    uv = jnp.transpose(w_uv, (1, 0, 2))
    w_uv_bd = jnp.where(same_head, uv[:, :, None, :], 0.0).reshape(MLA_H * KV_LORA, MLA_H * MLA_V).astype(BF16)
    grp = jnp.arange(DIFF_W) // (2 * DIFF_D)
    bd64 = jnp.where(grp[:, None] == grp[None, :], 1.0 / (2 * DIFF_D), 0.0).astype(BF16)
    return {
        'g_attn': g_attn_norm.reshape(1, D_MODEL), 'w_a': w_a, 'w_g': w_g,
        'g_q_a': g_q_a.reshape(1, Q_LORA), 'w_uq': w_uq_p, 'w_n': w_n, 'w_e': w_e,
        'g_kv_a': g_kv_a.reshape(1, KV_LORA), 'w_uv_bd': w_uv_bd, 'w_o_mla': w_o_mla.astype(BF16),
        'bd64': bd64, 'g_subln': jnp.tile(g_subln, DIFF_H).reshape(1, DIFF_W),
        'w_o_diff': w_o_diff.astype(BF16), 'w_out': w_out.astype(BF16),
        'g_ffn': g_ffn_norm.reshape(1, D_MODEL), 'rw_t': router_w.T.astype(BF16),
        'rb': jnp.broadcast_to(router_b.reshape(N_EXPERTS, 1), (N_EXPERTS, LANES)),
    }


def _pad_keys(a, tk):
    lk = a.shape[1]
    pad = (-lk) % tk
    if pad == 0:
        return a
    return jnp.concatenate([a, jnp.zeros((a.shape[0], pad) + a.shape[2:], a.dtype)], axis=1)


def _slot_base(topi, pstart):
    e = jnp.arange(N_EXPERTS, dtype=jnp.int32)
    return jnp.sum(jnp.where(topi[..., None] == e, pstart, 0), axis=-1)


def kernel(x_prompt, x_sample, cache_mla_ckv, cache_mla_kpe, cache_diff_k, cache_diff_v, g_attn_norm, w_in, g_q_a, w_uq, g_kv_a, w_uk, w_uv, w_o_mla, lambda_q1, lambda_k1, lambda_q2, lambda_k2, g_subln, w_o_diff, w_out, g_ffn_norm, router_w, router_b, w_gate_up, b_gate_up, w_down, b_down, g_final):
    depth = w_in.shape[0]
    assert depth == 1
    li = 0
    lam_init = 0.8 - 0.6 * math.exp(-0.3 * li)
    bp, lp, d = x_prompt.shape
    bs, ls, _ = x_sample.shape
    past = cache_mla_ckv.shape[2]
    tk = 256

    w = _prep_weights(w_in[li], g_attn_norm[li], g_q_a[li], w_uq[li], g_kv_a[li], w_uk[li], w_uv[li], w_o_mla[li],
                      g_subln[li], w_o_diff[li], w_out[li], g_ffn_norm[li], router_w[li], router_b[li])
    lam_rows = jnp.stack([lambda_q1[li], lambda_k1[li], lambda_q2[li], lambda_k2[li]]).astype(F32)

    cos_p, sin_p = _rope_tables(jnp.arange(lp, dtype=jnp.int32))
    ckv_p, kpe_p, dk_p, dv_p, qm_p, kvm_p, dqb_p, dkb_p, dvb_p = _proj(x_prompt, cos_p, sin_p, w)
    olat_p, odiff_p = _attention(qm_p, dqb_p, kvm_p, dkb_p, dvb_p, lam_rows, q_pos0=0, n_keys=lp, lam_init=lam_init)

    cos_s, sin_s = _rope_tables(jnp.arange(past, past + ls, dtype=jnp.int32))
    ckv_s, kpe_s, dk_s, dv_s, qm_s, kvm_s, dqb_s, dkb_s, dvb_s = _proj(x_sample, cos_s, sin_s, w)
    kvm_past = jnp.concatenate([cache_mla_ckv[li], cache_mla_kpe[li],
                                jnp.zeros((bs, past, QK_PAD - KV_LORA - MLA_ROPE), F32)], axis=2).astype(BF16)
    kvm_all = _pad_keys(jnp.concatenate([kvm_past, kvm_s], axis=1), tk)
    dkb_all = _pad_keys(jnp.concatenate([cache_diff_k[li].reshape(bs, past, DIFF_W).astype(BF16), dkb_s], axis=1), tk)
    dvb_all = _pad_keys(jnp.concatenate([cache_diff_v[li].reshape(bs, past, DIFF_W).astype(BF16), dvb_s], axis=1), tk)
    olat_s, odiff_s = _attention(qm_s, dqb_s, kvm_all, dkb_all, dvb_all, lam_rows,
                                 q_pos0=past, n_keys=past + ls, lam_init=lam_init)

    tp, ts = bp * lp, bs * ls
    cnt0 = jnp.zeros((N_EXPERTS, LANES), F32)
    h_p, tok_p, topi_p, rank_p, gate_p, cnt1 = _post(x_prompt.reshape(tp, d), olat_p.reshape(tp, -1),
                                                     odiff_p.reshape(tp, -1), cnt0, w, lam_init=lam_init)
    h_s, tok_s, topi_s, rank_s, gate_s, cnt2 = _post(x_sample.reshape(ts, d), olat_s.reshape(ts, -1),
                                                     odiff_s.reshape(ts, -1), cnt1, w, lam_init=lam_init)

    r = 512
    n_assign = (tp + ts) * TOP_K
    n_blocks = -(-n_assign // r) + N_EXPERTS
    n_slots = n_blocks * r
    counts = cnt2[:, 0].astype(jnp.int32)
    padded = (counts + r - 1) // r * r
    pend = jnp.cumsum(padded)
    pstart = pend - padded
    blk_start = jnp.arange(n_blocks, dtype=jnp.int32) * r
    block_e = jnp.minimum(jnp.sum((pend[None, :] <= blk_start[:, None]).astype(jnp.int32), axis=1), N_EXPERTS - 1)
    seg_end = _slot_base(block_e, pstart + counts)
    n_valid = jnp.where(blk_start < pend[-1], jnp.clip(seg_end - blk_start, 0, r), 0).astype(jnp.int32)
    pos_p = rank_p + _slot_base(topi_p, pstart)
    pos_s = rank_s + _slot_base(topi_s, pstart)

    xs = _dispatch(pos_p, tok_p, None, n_slots)
    xs = _dispatch(pos_s, tok_s, xs, n_slots)
    ys = _experts(block_e, n_valid, xs, w_gate_up[li], b_gate_up[li].reshape(N_EXPERTS, 1, -1),
                  w_down[li], b_down[li].reshape(N_EXPERTS, 1, -1), r)
    gfin = g_final.reshape(1, d)
    y_p = _combine(pos_p, gate_p, h_p, gfin, ys)
    y_s = _combine(pos_s, gate_s, h_s, gfin, ys)

    def stack(a, shape):
        return a.reshape((1,) + shape)

    return (y_p.reshape(bp, lp, d), y_s.reshape(bs, ls, d),
            stack(ckv_p, (bp, lp, KV_LORA)), stack(kpe_p, (bp, lp, MLA_ROPE)),
            stack(dk_p, (bp, lp, DIFF_H, 2, DIFF_D)), stack(dv_p, (bp, lp, DIFF_H, 2 * DIFF_D)),
            stack(ckv_s, (bs, ls, KV_LORA)), stack(kpe_s, (bs, ls, MLA_ROPE)),
            stack(dk_s, (bs, ls, DIFF_H, 2, DIFF_D)), stack(dv_s, (bs, ls, DIFF_H, 2 * DIFF_D)))
```

```python
import functools
import math

import jax
import jax.numpy as jnp
from jax import lax
from jax.experimental import pallas as pl
from jax.experimental.pallas import tpu as pltpu

F32 = jnp.float32
BF16 = jnp.bfloat16

D_MODEL = 1024
CHUNK = 64
CHUNK_SHIFT = 6
ROPE_THETA = 10000.0
RMS_EPS = 1e-6
SUBLN_EPS = 1e-5
NEG_INF = -1e30

MLA_H = 8
Q_LORA = 256
KV_LORA = 128
MLA_NOPE = 64
MLA_ROPE = 32
MLA_QK = MLA_NOPE + MLA_ROPE
MLA_V = 64
MLA_SCALE = MLA_QK ** -0.5

DIFF_H = 8
DIFF_D = 32
DIFF_W = DIFF_H * 2 * DIFF_D
DIFF_SCALE = DIFF_D ** -0.5
N_PAIR = DIFF_W // 128
MAPS_PER_PAIR = 4

N_EXPERTS = 32
TOP_K = 4
D_FF = D_MODEL
SWIGLU_ALPHA = 1.702
SWIGLU_LIMIT = 7.0

LANES = 128
QK_PAD = 256
LOG2E = math.log2(math.e)
VMEM_LIMIT = 56 * 1024 * 1024
N_DMA_PRIORITIES = 2
MERGE_ROWS = 256

A_QD = 0
A_KV = A_QD + Q_LORA
A_DQ = A_KV + KV_LORA
A_DK = A_DQ + DIFF_W
A_DV = A_DK + DIFF_W
A_KR = A_DV + DIFF_W
A_COLS = A_KR + LANES


def _rms(x, g, eps):
    return x * lax.rsqrt(jnp.mean(x * x, axis=-1, keepdims=True) + eps) * g


def _dot(a, b):
    return jnp.dot(a, b, preferred_element_type=F32)


def _dot_nt(a, b):
    return lax.dot_general(a, b, (((1,), (1,)), ((), ())), preferred_element_type=F32)


def _pack_bf16_pairs(x):
    w = x.shape[1] // 2
    lo = lax.bitcast_convert_type(x[:, :w].astype(F32), jnp.uint32)
    hi = lax.bitcast_convert_type(x[:, w:].astype(F32), jnp.uint32)
    return lax.shift_right_logical(lo, jnp.uint32(16)) | (hi & jnp.uint32(0xFFFF0000))


def _unpack_bf16_pairs(p):
    lo = lax.bitcast_convert_type(lax.shift_left(p, jnp.uint32(16)), F32)
    hi = lax.bitcast_convert_type(p & jnp.uint32(0xFFFF0000), F32)
    return jnp.concatenate([lo, hi], axis=1).astype(BF16)


def _proj_kernel(x_ref, cos_ref, sin_ref, gattn_ref, wa_ref, gqa_ref, wuq_ref, wn_ref, we_ref, gkva_ref,
                 ckv_ref, kpe_ref, dk_ref, dv_ref, qm_ref, kvm_ref, dqb_ref, dkb_ref, dvb_ref):
    tm = x_ref.shape[1]
    x = x_ref[0]
    xn = _rms(x, gattn_ref[...], RMS_EPS).astype(BF16)
    p = _dot(xn, wa_ref[...])

    cos = cos_ref[...]
    sin = sin_ref[...]
    lane = lax.broadcasted_iota(jnp.int32, (tm, LANES), 1)
    first_half = (lane & (DIFF_D - 1)) < (DIFF_D // 2)

    def rope128(xc):
        fwd = pltpu.roll(xc, LANES - DIFF_D // 2, 1)
        bwd = pltpu.roll(xc, DIFF_D // 2, 1)
        return xc * cos + jnp.where(first_half, fwd, bwd) * sin

    def rope_wide(xw):
        return jnp.concatenate([rope128(xw[:, c * LANES:(c + 1) * LANES]) for c in range(xw.shape[1] // LANES)], axis=1)

    cq = _rms(p[:, A_QD:A_QD + Q_LORA], gqa_ref[...], RMS_EPS).astype(BF16)
    q = _dot(cq, wuq_ref[...])
    qn = q[:, :MLA_H * MLA_NOPE].astype(BF16)
    qr = rope_wide(q[:, MLA_H * MLA_NOPE:]).astype(BF16)
    q_lat = _dot(qn, wn_ref[...])
    q_pe = _dot(qr, we_ref[...])
    for h in range(MLA_H):
        cols = slice(h * KV_LORA, (h + 1) * KV_LORA)
        qe = jnp.concatenate([q_lat[:, cols], q_pe[:, cols]], axis=1)
        qm_ref[0, h] = (qe * (MLA_SCALE * LOG2E)).astype(BF16)

    ckv = _rms(p[:, A_KV:A_KV + KV_LORA], gkva_ref[...], RMS_EPS)
    ckv_ref[0] = ckv
    kseg = rope128(p[:, A_KR:A_KR + LANES])
    kpe_ref[0] = kseg[:, :MLA_ROPE]
    kvm_ref[0] = jnp.concatenate([ckv, kseg], axis=1).astype(BF16)

    dq = rope_wide(p[:, A_DQ:A_DQ + DIFF_W])
    dqb_ref[0] = (dq * (DIFF_SCALE * LOG2E)).astype(BF16)
    dk = rope_wide(p[:, A_DK:A_DK + DIFF_W])
    dk_ref[0] = dk
    dkb_ref[0] = dk.astype(BF16)
    dv = p[:, A_DV:A_DV + DIFF_W]
    dv_ref[0] = dv
    dvb_ref[0] = dv.astype(BF16)


def _proj(x, cos, sin, w):
    b, l, d = x.shape
    tm = min(512, l)
    assert l % tm == 0
    grid = (l // tm, b)

    def full(a):
        return pl.BlockSpec(a.shape, lambda i, j, n=a.ndim: (0,) * n)

    def tok(width):
        return pl.BlockSpec((1, tm, width), lambda i, j: (j, i, 0))

    weights = (w['g_attn'], w['w_a'], w['g_q_a'], w['w_uq'], w['w_n'], w['w_e'], w['g_kv_a'])
    out_shape = (
        jax.ShapeDtypeStruct((b, l, KV_LORA), F32),
        jax.ShapeDtypeStruct((b, l, MLA_ROPE), F32),
        jax.ShapeDtypeStruct((b, l, DIFF_W), F32),
        jax.ShapeDtypeStruct((b, l, DIFF_W), F32),
        jax.ShapeDtypeStruct((b, MLA_H, l, QK_PAD), BF16),
        jax.ShapeDtypeStruct((b, l, QK_PAD), BF16),
        jax.ShapeDtypeStruct((b, l, DIFF_W), BF16),
        jax.ShapeDtypeStruct((b, l, DIFF_W), BF16),
        jax.ShapeDtypeStruct((b, l, DIFF_W), BF16),
    )
    out_specs = (tok(KV_LORA), tok(MLA_ROPE), tok(DIFF_W), tok(DIFF_W),
                 pl.BlockSpec((1, MLA_H, tm, QK_PAD), lambda i, j: (j, 0, i, 0)),
                 tok(QK_PAD), tok(DIFF_W), tok(DIFF_W), tok(DIFF_W))
    tab = pl.BlockSpec((tm, LANES), lambda i, j: (i, 0))
    return pl.pallas_call(
        _proj_kernel,
        grid=grid,
        in_specs=[tok(d), tab, tab] + [full(a) for a in weights],
        out_specs=out_specs,
        out_shape=out_shape,
        compiler_params=pltpu.CompilerParams(dimension_semantics=("arbitrary", "arbitrary"),
                                             vmem_limit_bytes=VMEM_LIMIT),
        name="proj",
    )(x, cos, sin, *weights)


def _attn_kernel(qm_ref, dq_ref, kvm_ref, dk_ref, dv_ref, lam_ref, olat_ref, odiff_ref,
                 qd_ref, m_ref, l_ref, acc_ref, *, tq, tk, q_pos0, n_keys, lam_init):
    i = pl.program_id(1)

    lane = lax.broadcasted_iota(jnp.int32, (tq, LANES), 1)
    seg = lax.shift_right_logical(lane, 5)
    for p in range(N_PAIR):
        qp = dq_ref[0, :, p * LANES:(p + 1) * LANES].astype(F32)
        for j in range(MAPS_PER_PAIR):
            blk = MAPS_PER_PAIR * p + j
            qd_ref[blk * tq:(blk + 1) * tq] = jnp.where(seg == j, qp, 0.0).astype(BF16)

    q_lo = q_pos0 + i * tq
    q_hi = q_lo + tq - 1
    vis_all = jnp.minimum((q_lo // CHUNK + 1) * CHUNK, n_keys)
    vis_any = jnp.minimum((q_hi // CHUNK + 1) * CHUNK, n_keys)
    n_full = vis_all // tk
    n_tot = (vis_any + tk - 1) // tk

    merge = max(1, MERGE_ROWS // tq)
    g_mla, g_diff = min(MLA_H, merge), min(MAPS_PER_PAIR, merge)
    work = [(b, g_mla) for b in range(0, MLA_H, g_mla)]
    work += [(MLA_H + b, g_diff) for b in range(0, MAPS_PER_PAIR * N_PAIR, g_diff)]

    def step(start, width, masked, first=False):
        if masked:
            k_pos = start + lax.broadcasted_iota(jnp.int32, (1, width), 1)
            k_chunk = jnp.where(k_pos < n_keys, lax.shift_right_logical(k_pos, CHUNK_SHIFT), jnp.int32(2 ** 30))
        for blk, g in work:
            n_rows = g * tq
            rows = slice(blk * tq, blk * tq + n_rows)
            if blk < MLA_H:
                q = qm_ref[0, blk] if g == 1 else qm_ref[0, blk:blk + g].reshape(n_rows, QK_PAD)
                k = kvm_ref[0, pl.ds(start, width), :]
                v = kvm_ref[0, pl.ds(start, width), 0:KV_LORA]
            else:
                p = (blk - MLA_H) // MAPS_PER_PAIR
                q = qd_ref[(blk - MLA_H) * tq:(blk - MLA_H) * tq + n_rows]
                k = dk_ref[0, pl.ds(start, width), p * LANES:(p + 1) * LANES]
                v = dv_ref[0, pl.ds(start, width), p * LANES:(p + 1) * LANES]
            s = _dot_nt(q, k)
            if masked:
                q_row = lax.broadcasted_iota(jnp.int32, (n_rows, 1), 0) & (tq - 1)
                q_chunk = lax.shift_right_logical(q_lo + q_row, CHUNK_SHIFT)
                s = jnp.where(k_chunk <= q_chunk, s, NEG_INF)
            s_max = jnp.max(s, axis=1, keepdims=True)
            if first:
                m_new = jnp.broadcast_to(s_max, (n_rows, LANES))
            else:
                m_old = m_ref[rows]
                m_new = jnp.maximum(m_old, s_max)
                alpha = jnp.exp2(m_old - m_new)
            ps = [jnp.exp2(s[:, c * LANES:(c + 1) * LANES] - m_new) for c in range(width // LANES)]
            pv = _dot(jnp.concatenate(ps, axis=1).astype(BF16), v)
            if first:
                l_ref[rows] = sum(ps)
                acc_ref[rows] = pv
            else:
                l_ref[rows] = alpha * l_ref[rows] + sum(ps)
                acc_ref[rows] = alpha * acc_ref[rows] + pv
            m_ref[rows] = m_new

    def full_steps(n, base):
        def body(j, c):
            for u in range(n):
                step(pl.multiple_of((base + n * j + u) * tk, tk), tk, False)
            return c
        return body

    def masked_step(kt, c):
        step(pl.multiple_of(kt * tk, tk), tk, True)
        return c

    @pl.when(n_full > 0)
    def _():
        step(0, tk, False, first=True)

    @pl.when(n_full == 0)
    def _():
        step(0, tk, True, first=True)

    n_rest = jnp.maximum(n_full - 1, 0)
    c4 = n_rest // 4
    c2 = (n_rest - 4 * c4) // 2
    c1 = n_rest - 4 * c4 - 2 * c2
    lax.fori_loop(0, c4, full_steps(4, 1), 0)
    lax.fori_loop(0, c2, full_steps(2, 1 + 4 * c4), 0)
    lax.fori_loop(0, c1, full_steps(1, 1 + 4 * c4 + 2 * c2), 0)
    lax.fori_loop(jnp.maximum(n_full, 1), n_tot, masked_step, 0)

    def normalised(blk):
        rows = slice(blk * tq, (blk + 1) * tq)
        return acc_ref[rows] / jnp.sum(l_ref[rows], axis=1, keepdims=True)

    for h in range(MLA_H):
        olat_ref[0, :, h * LANES:(h + 1) * LANES] = normalised(h).astype(BF16)

    lam = (jnp.exp(jnp.sum(lam_ref[0:1] * lam_ref[1:2], axis=1, keepdims=True))
           - jnp.exp(jnp.sum(lam_ref[2:3] * lam_ref[3:4], axis=1, keepdims=True)) + lam_init)
    for p in range(N_PAIR):
        n = [normalised(MLA_H + MAPS_PER_PAIR * p + j) for j in range(MAPS_PER_PAIR)]
        head0 = n[0] - lam * n[1]
        head1 = n[2] - lam * n[3]
        odiff_ref[0, :, p * LANES:(p + 1) * LANES] = jnp.where(lane < 2 * DIFF_D, head0, head1)


def _attention(qm, dqb, kvm, dkb, dvb, lam_rows, *, q_pos0, n_keys, lam_init):
    b, _, l, _ = qm.shape
    lk = kvm.shape[1]
    tq = min(256, l)
    tk = 256
    assert l % tq == 0 and lk % tk == 0 and tq & (tq - 1) == 0
    n_rows = (MLA_H + MAPS_PER_PAIR * N_PAIR) * tq
    kern = functools.partial(_attn_kernel, tq=tq, tk=tk, q_pos0=q_pos0, n_keys=n_keys, lam_init=lam_init)
    per_batch = lambda width: pl.BlockSpec((1, lk, width), lambda j, i: (j, 0, 0))
    return pl.pallas_call(
        kern,
        grid=(b, l // tq),
        in_specs=[pl.BlockSpec((1, MLA_H, tq, QK_PAD), lambda j, i: (j, 0, i, 0)),
                  pl.BlockSpec((1, tq, DIFF_W), lambda j, i: (j, i, 0)),
                  per_batch(QK_PAD), per_batch(DIFF_W), per_batch(DIFF_W),
                  pl.BlockSpec(lam_rows.shape, lambda j, i: (0, 0))],
        out_specs=(pl.BlockSpec((1, tq, MLA_H * KV_LORA), lambda j, i: (j, i, 0)),
                   pl.BlockSpec((1, tq, DIFF_W), lambda j, i: (j, i, 0))),
        out_shape=(jax.ShapeDtypeStruct((b, l, MLA_H * KV_LORA), BF16),
                   jax.ShapeDtypeStruct((b, l, DIFF_W), F32)),
        scratch_shapes=[pltpu.VMEM((MAPS_PER_PAIR * N_PAIR * tq, LANES), BF16),
                        pltpu.VMEM((n_rows, LANES), F32),
                        pltpu.VMEM((n_rows, LANES), F32),
                        pltpu.VMEM((n_rows, LANES), F32)],
        compiler_params=pltpu.CompilerParams(dimension_semantics=("arbitrary", "arbitrary"),
                                             vmem_limit_bytes=VMEM_LIMIT),
        name="attn",
    )(qm, dqb, kvm, dkb, dvb, lam_rows)


def _post_kernel(x_ref, olat_ref, odiff_ref, cnt_in_ref, su_ref, gattn_ref, wg_ref, wuv_ref, womla_ref, bd_ref,
                 gsub_ref, wodiff_ref, wout_ref, gffn_ref, rwt_ref, rb_ref,
                 h_ref, tok_ref, topi_ref, rank_ref, gate_ref, cnt_ref, *, lam_init):
    tm = x_ref.shape[0]
    x = x_ref[...]
    xn = _rms(x, gattn_ref[...], RMS_EPS).astype(BF16)
    g = _dot(xn, wg_ref[...])
    y_mla = _dot(_dot(olat_ref[...], wuv_ref[...]).astype(BF16), womla_ref[...])

    od = odiff_ref[...]
    sq = od * od
    sq_hi = sq.astype(BF16)
    sq_lo = (sq - sq_hi.astype(F32)).astype(BF16)
    ms = _dot(sq_hi, bd_ref[...]) + _dot(sq_lo, bd_ref[...])
    on = od * lax.rsqrt(ms + SUBLN_EPS) * gsub_ref[...] * (1.0 - lam_init)
    y_diff = _dot(on.astype(BF16), wodiff_ref[...])

    merged = jax.nn.sigmoid(g[:, :D_MODEL]) * y_mla + jax.nn.sigmoid(g[:, D_MODEL:]) * y_diff
    h = x + _dot(merged.astype(BF16), wout_ref[...])
    h_ref[...] = h
    tok = _rms(h, gffn_ref[...], RMS_EPS).astype(BF16)
    tok_ref[...] = _pack_bf16_pairs(tok)

    logits = _dot_nt(rwt_ref[...], tok) + rb_ref[:, 0:1]
    e_iota = lax.broadcasted_iota(jnp.int32, (N_EXPERTS, tm), 0).astype(F32)
    work = logits
    vals, hots = [], []
    for k in range(TOP_K):
        mx = jnp.max(work, axis=0, keepdims=True)
        idx = jnp.min(jnp.where(work == mx, e_iota, float(N_EXPERTS)), axis=0, keepdims=True)
        hot = e_iota == idx
        work = jnp.where(hot, -jnp.inf, work)
        vals.append(mx)
        hots.append(hot)
        topi_ref[k:k + 1, :] = idx.astype(jnp.int32)
    exps = [jnp.exp(v - vals[0]) for v in vals]
    den = exps[0] + exps[1] + exps[2] + exps[3]
    for k in range(TOP_K):
        gate_ref[k:k + 1, :] = exps[k] / den

    @pl.when(pl.program_id(0) == 0)
    def _():
        cnt_ref[...] = cnt_in_ref[...]

    sel = sum(jnp.where(hot, 1.0, 0.0) for hot in hots)
    before = _dot(sel.astype(BF16), su_ref[...]) + cnt_ref[:, 0:1]
    for k in range(TOP_K):
        rank_ref[k:k + 1, :] = jnp.sum(jnp.where(hots[k], before, 0.0), axis=0, keepdims=True).astype(jnp.int32)
    cnt_ref[...] = cnt_ref[...] + jnp.sum(sel, axis=1, keepdims=True)


def _post(x2d, olat2d, odiff2d, cnt_in, w, *, lam_init):
    t, d = x2d.shape
    tm = min(512, t)
    assert t % tm == 0
    su = (lax.broadcasted_iota(jnp.int32, (tm, tm), 0) < lax.broadcasted_iota(jnp.int32, (tm, tm), 1)).astype(BF16)
    weights = (w['g_attn'], w['w_g'], w['w_uv_bd'], w['w_o_mla'], w['bd64'], w['g_subln'], w['w_o_diff'],
               w['w_out'], w['g_ffn'], w['rw_t'], w['rb'])

    def full(a):
        return pl.BlockSpec(a.shape, lambda i, n=a.ndim: (0,) * n)

    tokspec = lambda width: pl.BlockSpec((tm, width), lambda i: (i, 0))
    small = pl.BlockSpec((TOP_K, tm), lambda i: (0, i))
    kern = functools.partial(_post_kernel, lam_init=lam_init)
    return pl.pallas_call(
        kern,
        grid=(t // tm,),
        in_specs=[tokspec(d), tokspec(MLA_H * KV_LORA), tokspec(DIFF_W), full(cnt_in), full(su)]
                 + [full(a) for a in weights],
        out_specs=(tokspec(d), tokspec(d // 2), small, small, small, full(cnt_in)),
        out_shape=(jax.ShapeDtypeStruct((t, d), F32), jax.ShapeDtypeStruct((t, d // 2), jnp.uint32),
                   jax.ShapeDtypeStruct((TOP_K, t), jnp.int32), jax.ShapeDtypeStruct((TOP_K, t), jnp.int32),
                   jax.ShapeDtypeStruct((TOP_K, t), F32), jax.ShapeDtypeStruct(cnt_in.shape, F32)),
        compiler_params=pltpu.CompilerParams(dimension_semantics=("arbitrary",), vmem_limit_bytes=VMEM_LIMIT),
        name="post",
    )(x2d, olat2d, odiff2d, cnt_in, su, *weights)


def _dispatch_kernel(pos_ref, tok_ref, *rest):
    xs_ref, sems = rest[-2], rest[-1]
    tm = tok_ref.shape[0]

    for t in range(tm):
        for k in range(TOP_K):
            q = k % N_DMA_PRIORITIES
            pltpu.make_async_copy(tok_ref.at[pl.ds(t, 1)], xs_ref.at[pl.ds(pos_ref[k, t], 1)],
                                  sems.at[q]).start(priority=q)
    for k in range(TOP_K):
        pltpu.make_async_copy(tok_ref, xs_ref.at[pl.ds(0, tm)], sems.at[k % N_DMA_PRIORITIES]).wait()


def _dispatch(pos, tok, xs, n_slots):
    t, d = tok.shape
    tm = min(256, t)
    assert t % tm == 0
    in_specs = [pl.BlockSpec((TOP_K, tm), lambda i: (0, i), memory_space=pltpu.SMEM),
                pl.BlockSpec((tm, d), lambda i: (i, 0))]
    args = [pos, tok]
    aliases = {}
    if xs is not None:
        in_specs.append(pl.BlockSpec(memory_space=pl.ANY))
        args.append(xs)
        aliases = {2: 0}
    return pl.pallas_call(
        _dispatch_kernel,
        grid=(t // tm,),
        in_specs=in_specs,
        out_specs=pl.BlockSpec(memory_space=pl.ANY),
        out_shape=jax.ShapeDtypeStruct((n_slots, d), tok.dtype),
        scratch_shapes=[pltpu.SemaphoreType.DMA((N_DMA_PRIORITIES,))],
        input_output_aliases=aliases,
        compiler_params=pltpu.CompilerParams(dimension_semantics=("arbitrary",), has_side_effects=True),
        name="dispatch",
    )(*args)


def _expert_kernel(be_ref, nv_ref, xs_ref, wgu_ref, bgu_ref, wd_ref, bd_ref, ys_ref, wgu_bf_ref, wd_bf_ref):
    i = pl.program_id(0)
    r = xs_ref.shape[0]
    nv = nv_ref[i]

    @pl.when(nv > 0)
    def _():
        @pl.when(jnp.logical_or(i == 0, be_ref[i] != be_ref[jnp.maximum(i - 1, 0)]))
        def _():
            wgu_bf_ref[...] = wgu_ref[0].astype(BF16)
            wd_bf_ref[...] = wd_ref[0].astype(BF16)

        rows = lax.broadcasted_iota(jnp.int32, (r, 1), 0)
        x = _unpack_bf16_pairs(jnp.where(rows < nv, xs_ref[...], jnp.uint32(0)))
        h = _dot(x, wgu_bf_ref[...]) + bgu_ref[0]
        gate = jnp.minimum(h[:, :D_FF], SWIGLU_LIMIT)
        up = jnp.clip(h[:, D_FF:], -SWIGLU_LIMIT, SWIGLU_LIMIT)
        act = gate * jax.nn.sigmoid(SWIGLU_ALPHA * gate) * (up + 1.0)
        ys_ref[...] = _dot(act.astype(BF16), wd_bf_ref[...]) + bd_ref[0]


def _experts(block_e, n_valid, xs, wgu, bgu, wd, bd, rows_per_block):
    n_slots = xs.shape[0]
    d = D_MODEL
    r = rows_per_block
    n_blocks = n_slots // r
    grid_spec = pltpu.PrefetchScalarGridSpec(
        num_scalar_prefetch=2,
        grid=(n_blocks,),
        in_specs=[pl.BlockSpec((r, xs.shape[1]), lambda i, be, nv: (i, 0)),
                  pl.BlockSpec((1, d, 2 * D_FF), lambda i, be, nv: (be[i], 0, 0)),
                  pl.BlockSpec((1, 1, 2 * D_FF), lambda i, be, nv: (be[i], 0, 0)),
                  pl.BlockSpec((1, D_FF, d), lambda i, be, nv: (be[i], 0, 0)),
                  pl.BlockSpec((1, 1, d), lambda i, be, nv: (be[i], 0, 0))],
        out_specs=pl.BlockSpec((r, d), lambda i, be, nv: (i, 0)),
        scratch_shapes=[pltpu.VMEM((d, 2 * D_FF), BF16), pltpu.VMEM((D_FF, d), BF16)],
    )
    return pl.pallas_call(
        _expert_kernel,
        grid_spec=grid_spec,
        out_shape=jax.ShapeDtypeStruct((n_slots, d), F32),
        compiler_params=pltpu.CompilerParams(dimension_semantics=("arbitrary",), vmem_limit_bytes=VMEM_LIMIT),
        name="experts",
    )(block_e, n_valid, xs, wgu, bgu, wd, bd)


def _combine_kernel(pos_ref, pos_next_ref, gate_ref, h_ref, gfin_ref, ys_ref, out_ref, buf0_ref, buf1_ref, sems):
    tm = h_ref.shape[0]
    i = pl.program_id(0)
    last = pl.num_programs(0) - 1
    slot = i % 2
    bufs = (buf0_ref, buf1_ref)

    def gather(idx_ref, s):
        for t in range(tm):
            for k in range(TOP_K):
                q = k % N_DMA_PRIORITIES
                pltpu.make_async_copy(ys_ref.at[pl.ds(idx_ref[k, t], 1)], bufs[s].at[k, pl.ds(t, 1)],
                                      sems.at[s, q]).start(priority=q)

    def wait_tile(s):
        for k in range(TOP_K):
            pltpu.make_async_copy(ys_ref.at[pl.ds(0, tm)], bufs[s].at[k], sems.at[s, k % N_DMA_PRIORITIES]).wait()

    @pl.when(i == 0)
    def _():
        gather(pos_ref, 0)

    def tile(s):
        wait_tile(s)
        gather(pos_next_ref, 1 - s)
        eye = lax.broadcasted_iota(jnp.int32, (tm, tm), 0) == lax.broadcasted_iota(jnp.int32, (tm, tm), 1)
        acc = h_ref[...]
        for k in range(TOP_K):
            g_col = jnp.sum(jnp.where(eye, gate_ref[k:k + 1, :], 0.0), axis=1, keepdims=True)
            acc = acc + g_col * bufs[s][k]
        out_ref[...] = _rms(acc, gfin_ref[...], RMS_EPS)

        @pl.when(i == last)
        def _():
            wait_tile(1 - s)

    for s in range(2):
        pl.when(slot == s)(functools.partial(tile, s))


def _combine(pos, gate, h, g_final, ys):
    t, d = h.shape
    tm = min(256, t)
    assert t % tm == 0
    n_tiles = t // tm
    return pl.pallas_call(
        _combine_kernel,
        grid=(n_tiles,),
        in_specs=[pl.BlockSpec((TOP_K, tm), lambda i: (0, i), memory_space=pltpu.SMEM),
                  pl.BlockSpec((TOP_K, tm), lambda i: (0, jnp.minimum(i + 1, n_tiles - 1)), memory_space=pltpu.SMEM),
                  pl.BlockSpec((TOP_K, tm), lambda i: (0, i)),
                  pl.BlockSpec((tm, d), lambda i: (i, 0)),
                  pl.BlockSpec(g_final.shape, lambda i: (0, 0)),
                  pl.BlockSpec(memory_space=pl.ANY)],
        out_specs=pl.BlockSpec((tm, d), lambda i: (i, 0)),
        out_shape=jax.ShapeDtypeStruct((t, d), F32),
        scratch_shapes=[pltpu.VMEM((TOP_K, tm, d), F32), pltpu.VMEM((TOP_K, tm, d), F32),
                        pltpu.SemaphoreType.DMA((2, N_DMA_PRIORITIES))],
        compiler_params=pltpu.CompilerParams(dimension_semantics=("arbitrary",), vmem_limit_bytes=VMEM_LIMIT),
        name="combine",
    )(pos, pos, gate, h, g_final, ys)


def _rope_tables(pos):
    half = DIFF_D // 2
    inv = ROPE_THETA ** (-jnp.arange(half, dtype=F32) / half)
    ang = pos.astype(F32)[:, None] * inv[None, :]
    cos, sin = jnp.cos(ang), jnp.sin(ang)
    reps = LANES // DIFF_D
    return (jnp.tile(jnp.concatenate([cos, cos], axis=1), (1, reps)),
            jnp.tile(jnp.concatenate([-sin, sin], axis=1), (1, reps)))


def _prep_weights(w_in, g_attn_norm, g_q_a, w_uq, g_kv_a, w_uk, w_uv, w_o_mla, g_subln, w_o_diff, w_out,
                  g_ffn_norm, router_w, router_b):
    o_q, o_kv, o_kr, o_dq, o_dk, o_dv, o_g = 0, 256, 384, 416, 928, 1440, 1952
    w_a = jnp.concatenate([w_in[:, o_q:o_kv], w_in[:, o_kv:o_kr], w_in[:, o_dq:o_dk], w_in[:, o_dk:o_dv],
                           w_in[:, o_dv:o_g], w_in[:, o_kr:o_dq],
                           jnp.zeros((D_MODEL, LANES - MLA_ROPE), w_in.dtype)], axis=1).astype(BF16)
    w_g = w_in[:, o_g:].astype(BF16)
    uq = w_uq.reshape(Q_LORA, MLA_H, MLA_QK)
    w_uq_p = jnp.concatenate([uq[:, :, :MLA_NOPE].reshape(Q_LORA, MLA_H * MLA_NOPE),
                              uq[:, :, MLA_NOPE:].reshape(Q_LORA, MLA_H * MLA_ROPE)], axis=1).astype(BF16)
    same_head = jnp.eye(MLA_H, dtype=bool)[:, None, :, None]
    ukt = jnp.transpose(w_uk, (1, 2, 0))
    w_n = jnp.where(same_head, ukt[:, :, None, :], 0.0).reshape(MLA_H * MLA_NOPE, MLA_H * KV_LORA).astype(BF16)
    place = jnp.eye(MLA_ROPE, KV_LORA, dtype=F32)
    w_e = jnp.where(same_head, place[None, :, None, :], 0.0).reshape(MLA_H * MLA_ROPE, MLA_H * KV_LORA).astype(BF16)
    uv = jnp.transpose(w_uv, (1, 0, 2))
    w_uv_bd = jnp.where(same_head, uv[:, :, None, :], 0.0).reshape(MLA_H * KV_LORA, MLA_H * MLA_V).astype(BF16)
    grp = jnp.arange(DIFF_W) // (2 * DIFF_D)
    bd64 = jnp.where(grp[:, None] == grp[None, :], 1.0 / (2 * DIFF_D), 0.0).astype(BF16)
    return {
        'g_attn': g_attn_norm.reshape(1, D_MODEL), 'w_a': w_a, 'w_g': w_g,
        'g_q_a': g_q_a.reshape(1, Q_LORA), 'w_uq': w_uq_p, 'w_n': w_n, 'w_e': w_e,
        'g_kv_a': g_kv_a.reshape(1, KV_LORA), 'w_uv_bd': w_uv_bd, 'w_o_mla': w_o_mla.astype(BF16),
        'bd64': bd64, 'g_subln': jnp.tile(g_subln, DIFF_H).reshape(1, DIFF_W),
        'w_o_diff': w_o_diff.astype(BF16), 'w_out': w_out.astype(BF16),
        'g_ffn': g_ffn_norm.reshape(1, D_MODEL), 'rw_t': router_w.T.astype(BF16),
        'rb': jnp.broadcast_to(router_b.reshape(N_EXPERTS, 1), (N_EXPERTS, LANES)),
    }


def _pad_keys(a, tk):
    lk = a.shape[1]
    pad = (-lk) % tk
    if pad == 0:
        return a
    return jnp.concatenate([a, jnp.zeros((a.shape[0], pad) + a.shape[2:], a.dtype)], axis=1)


def _slot_base(topi, pstart):
    e = jnp.arange(N_EXPERTS, dtype=jnp.int32)
    return jnp.sum(jnp.where(topi[..., None] == e, pstart, 0), axis=-1)


def kernel(x_prompt, x_sample, cache_mla_ckv, cache_mla_kpe, cache_diff_k, cache_diff_v, g_attn_norm, w_in, g_q_a, w_uq, g_kv_a, w_uk, w_uv, w_o_mla, lambda_q1, lambda_k1, lambda_q2, lambda_k2, g_subln, w_o_diff, w_out, g_ffn_norm, router_w, router_b, w_gate_up, b_gate_up, w_down, b_down, g_final):
    depth = w_in.shape[0]
    assert depth == 1
    li = 0
    lam_init = 0.8 - 0.6 * math.exp(-0.3 * li)
    bp, lp, d = x_prompt.shape
    bs, ls, _ = x_sample.shape
    past = cache_mla_ckv.shape[2]
    tk = 256

    w = _prep_weights(w_in[li], g_attn_norm[li], g_q_a[li], w_uq[li], g_kv_a[li], w_uk[li], w_uv[li], w_o_mla[li],
                      g_subln[li], w_o_diff[li], w_out[li], g_ffn_norm[li], router_w[li], router_b[li])
    lam_rows = jnp.stack([lambda_q1[li], lambda_k1[li], lambda_q2[li], lambda_k2[li]]).astype(F32)

    cos_p, sin_p = _rope_tables(jnp.arange(lp, dtype=jnp.int32))
    ckv_p, kpe_p, dk_p, dv_p, qm_p, kvm_p, dqb_p, dkb_p, dvb_p = _proj(x_prompt, cos_p, sin_p, w)
    olat_p, odiff_p = _attention(qm_p, dqb_p, kvm_p, dkb_p, dvb_p, lam_rows, q_pos0=0, n_keys=lp, lam_init=lam_init)

    cos_s, sin_s = _rope_tables(jnp.arange(past, past + ls, dtype=jnp.int32))
    ckv_s, kpe_s, dk_s, dv_s, qm_s, kvm_s, dqb_s, dkb_s, dvb_s = _proj(x_sample, cos_s, sin_s, w)
    kvm_past = jnp.concatenate([cache_mla_ckv[li], cache_mla_kpe[li],
                                jnp.zeros((bs, past, QK_PAD - KV_LORA - MLA_ROPE), F32)], axis=2).astype(BF16)
    kvm_all = _pad_keys(jnp.concatenate([kvm_past, kvm_s], axis=1), tk)
    dkb_all = _pad_keys(jnp.concatenate([cache_diff_k[li].reshape(bs, past, DIFF_W).astype(BF16), dkb_s], axis=1), tk)
    dvb_all = _pad_keys(jnp.concatenate([cache_diff_v[li].reshape(bs, past, DIFF_W).astype(BF16), dvb_s], axis=1), tk)
    olat_s, odiff_s = _attention(qm_s, dqb_s, kvm_all, dkb_all, dvb_all, lam_rows,
                                 q_pos0=past, n_keys=past + ls, lam_init=lam_init)

    tp, ts = bp * lp, bs * ls
    cnt0 = jnp.zeros((N_EXPERTS, LANES), F32)
    h_p, tok_p, topi_p, rank_p, gate_p, cnt1 = _post(x_prompt.reshape(tp, d), olat_p.reshape(tp, -1),
                                                     odiff_p.reshape(tp, -1), cnt0, w, lam_init=lam_init)
    h_s, tok_s, topi_s, rank_s, gate_s, cnt2 = _post(x_sample.reshape(ts, d), olat_s.reshape(ts, -1),
                                                     odiff_s.reshape(ts, -1), cnt1, w, lam_init=lam_init)

    r = 512
    n_assign = (tp + ts) * TOP_K
    n_blocks = -(-n_assign // r) + N_EXPERTS
    n_slots = n_blocks * r
    counts = cnt2[:, 0].astype(jnp.int32)
    padded = (counts + r - 1) // r * r
    pend = jnp.cumsum(padded)
    pstart = pend - padded
    blk_start = jnp.arange(n_blocks, dtype=jnp.int32) * r
    block_e = jnp.minimum(jnp.sum((pend[None, :] <= blk_start[:, None]).astype(jnp.int32), axis=1), N_EXPERTS - 1)
    seg_end = _slot_base(block_e, pstart + counts)
    n_valid = jnp.where(blk_start < pend[-1], jnp.clip(seg_end - blk_start, 0, r), 0).astype(jnp.int32)
    pos_p = rank_p + _slot_base(topi_p, pstart)
    pos_s = rank_s + _slot_base(topi_s, pstart)

    xs = _dispatch(pos_p, tok_p, None, n_slots)
    xs = _dispatch(pos_s, tok_s, xs, n_slots)
    ys = _experts(block_e, n_valid, xs, w_gate_up[li], b_gate_up[li].reshape(N_EXPERTS, 1, -1),
                  w_down[li], b_down[li].reshape(N_EXPERTS, 1, -1), r)
    gfin = g_final.reshape(1, d)
    y_p = _combine(pos_p, gate_p, h_p, gfin, ys)
    y_s = _combine(pos_s, gate_s, h_s, gfin, ys)

    def stack(a, shape):
        return a.reshape((1,) + shape)

    return (y_p.reshape(bp, lp, d), y_s.reshape(bs, ls, d),
            stack(ckv_p, (bp, lp, KV_LORA)), stack(kpe_p, (bp, lp, MLA_ROPE)),
            stack(dk_p, (bp, lp, DIFF_H, 2, DIFF_D)), stack(dv_p, (bp, lp, DIFF_H, 2 * DIFF_D)),
            stack(ckv_s, (bs, ls, KV_LORA)), stack(kpe_s, (bs, ls, MLA_ROPE)),
            stack(dk_s, (bs, ls, DIFF_H, 2, DIFF_D)), stack(dv_s, (bs, ls, DIFF_H, 2 * DIFF_D)))
```

```python
import functools
import math

import jax
import jax.numpy as jnp
from jax import lax
from jax.experimental import pallas as pl
from jax.experimental.pallas import tpu as pltpu

F32 = jnp.float32
BF16 = jnp.bfloat16

D_MODEL = 1024
CHUNK = 64
CHUNK_SHIFT = 6
ROPE_THETA = 10000.0
RMS_EPS = 1e-6
SUBLN_EPS = 1e-5
NEG_INF = -1e30

MLA_H = 8
Q_LORA = 256
KV_LORA = 128
MLA_NOPE = 64
MLA_ROPE = 32
MLA_QK = MLA_NOPE + MLA_ROPE
MLA_V = 64
MLA_SCALE = MLA_QK ** -0.5

DIFF_H = 8
DIFF_D = 32
DIFF_W = DIFF_H * 2 * DIFF_D
DIFF_SCALE = DIFF_D ** -0.5
MAPS_PER_PAIR = 4
N_PAIR = DIFF_H * 2 // MAPS_PER_PAIR
DIFF_D_SHIFT = 5

N_EXPERTS = 32
TOP_K = 4
D_FF = D_MODEL
SWIGLU_ALPHA = 1.702
SWIGLU_LIMIT = 7.0

LANES = 128
MXU_DIM = 256

PROJ_ROWS = 512
POST_ROWS = 512
ATTN_Q_ROWS = MXU_DIM
ATTN_KEY_TILE = MXU_DIM
ROW_DMA_TILE = 256
EXPERT_BLOCK_ROWS = 512
QK_PAD = MXU_DIM
LOG2E = math.log2(math.e)
VMEM_LIMIT = 56 * 1024 * 1024
N_DMA_PRIORITIES = 2
MERGE_ROWS = MXU_DIM

A_QD = 0
A_KV = A_QD + Q_LORA
A_DQ = A_KV + KV_LORA
A_DK = A_DQ + DIFF_W
A_DV = A_DK + DIFF_W
A_KR = A_DV + DIFF_W
A_COLS = A_KR + LANES


def _rms(x, g, eps):
    return x * lax.rsqrt(jnp.mean(x * x, axis=-1, keepdims=True) + eps) * g


def _dot(a, b):
    return jnp.dot(a, b, preferred_element_type=F32)


def _dot_nt(a, b):
    return lax.dot_general(a, b, (((1,), (1,)), ((), ())), preferred_element_type=F32)


def _pack_bf16_pairs(x):
    w = x.shape[1] // 2
    lo = lax.bitcast_convert_type(x[:, :w].astype(F32), jnp.uint32)
    hi = lax.bitcast_convert_type(x[:, w:].astype(F32), jnp.uint32)
    return lax.shift_right_logical(lo, jnp.uint32(16)) | (hi & jnp.uint32(0xFFFF0000))


def _unpack_bf16_pairs(p):
    lo = lax.bitcast_convert_type(lax.shift_left(p, jnp.uint32(16)), F32)
    hi = lax.bitcast_convert_type(p & jnp.uint32(0xFFFF0000), F32)
    return jnp.concatenate([lo, hi], axis=1).astype(BF16)


def _proj_kernel(x_ref, cos_ref, sin_ref, gattn_ref, wa_ref, gqa_ref, wuq_ref, wn_ref, we_ref, gkva_ref,
                 ckv_ref, kpe_ref, dk_ref, dv_ref, qm_ref, kvm_ref, dqb_ref, dkb_ref, dvb_ref):
    tm = x_ref.shape[1]
    x = x_ref[0]
    xn = _rms(x, gattn_ref[...], RMS_EPS).astype(BF16)
    p = _dot(xn, wa_ref[...])

    cos = cos_ref[...]
    sin = sin_ref[...]
    lane = lax.broadcasted_iota(jnp.int32, (tm, LANES), 1)
    first_half = (lane & (DIFF_D - 1)) < (DIFF_D // 2)

    def rope128(xc):
        fwd = pltpu.roll(xc, LANES - DIFF_D // 2, 1)
        bwd = pltpu.roll(xc, DIFF_D // 2, 1)
        return xc * cos + jnp.where(first_half, fwd, bwd) * sin

    def rope_wide(xw):
        return jnp.concatenate([rope128(xw[:, c * LANES:(c + 1) * LANES]) for c in range(xw.shape[1] // LANES)], axis=1)

    cq = _rms(p[:, A_QD:A_QD + Q_LORA], gqa_ref[...], RMS_EPS).astype(BF16)
    q = _dot(cq, wuq_ref[...])
    qn = q[:, :MLA_H * MLA_NOPE].astype(BF16)
    qr = rope_wide(q[:, MLA_H * MLA_NOPE:]).astype(BF16)
    q_lat = _dot(qn, wn_ref[...])
    q_pe = _dot(qr, we_ref[...])
    for h in range(MLA_H):
        cols = slice(h * KV_LORA, (h + 1) * KV_LORA)
        qe = jnp.concatenate([q_lat[:, cols], q_pe[:, cols]], axis=1)
        qm_ref[0, h] = (qe * (MLA_SCALE * LOG2E)).astype(BF16)

    ckv = _rms(p[:, A_KV:A_KV + KV_LORA], gkva_ref[...], RMS_EPS)
    ckv_ref[0] = ckv
    kseg = rope128(p[:, A_KR:A_KR + LANES])
    kpe_ref[0] = kseg[:, :MLA_ROPE]
    kvm_ref[0] = jnp.concatenate([ckv, kseg], axis=1).astype(BF16)

    dq = rope_wide(p[:, A_DQ:A_DQ + DIFF_W])
    dqb_ref[0] = (dq * (DIFF_SCALE * LOG2E)).astype(BF16)
    dk = rope_wide(p[:, A_DK:A_DK + DIFF_W])
    dk_ref[0] = dk
    dkb_ref[0] = dk.astype(BF16)
    dv = p[:, A_DV:A_DV + DIFF_W]
    dv_ref[0] = dv
    dvb_ref[0] = dv.astype(BF16)


def _proj(x, cos, sin, w):
    b, l, d = x.shape
    tm = min(PROJ_ROWS, l)
    assert l % tm == 0
    grid = (l // tm, b)

    def full(a):
        return pl.BlockSpec(a.shape, lambda i, j, n=a.ndim: (0,) * n)

    def tok(width):
        return pl.BlockSpec((1, tm, width), lambda i, j: (j, i, 0))

    weights = (w['g_attn'], w['w_a'], w['g_q_a'], w['w_uq'], w['w_n'], w['w_e'], w['g_kv_a'])
    out_shape = (
        jax.ShapeDtypeStruct((b, l, KV_LORA), F32),
        jax.ShapeDtypeStruct((b, l, MLA_ROPE), F32),
        jax.ShapeDtypeStruct((b, l, DIFF_W), F32),
        jax.ShapeDtypeStruct((b, l, DIFF_W), F32),
        jax.ShapeDtypeStruct((b, MLA_H, l, QK_PAD), BF16),
        jax.ShapeDtypeStruct((b, l, QK_PAD), BF16),
        jax.ShapeDtypeStruct((b, l, DIFF_W), BF16),
        jax.ShapeDtypeStruct((b, l, DIFF_W), BF16),
        jax.ShapeDtypeStruct((b, l, DIFF_W), BF16),
    )
    out_specs = (tok(KV_LORA), tok(MLA_ROPE), tok(DIFF_W), tok(DIFF_W),
                 pl.BlockSpec((1, MLA_H, tm, QK_PAD), lambda i, j: (j, 0, i, 0)),
                 tok(QK_PAD), tok(DIFF_W), tok(DIFF_W), tok(DIFF_W))
    tab = pl.BlockSpec((tm, LANES), lambda i, j: (i, 0))
    return pl.pallas_call(
        _proj_kernel,
        grid=grid,
        in_specs=[tok(d), tab, tab] + [full(a) for a in weights],
        out_specs=out_specs,
        out_shape=out_shape,
        compiler_params=pltpu.CompilerParams(dimension_semantics=("arbitrary", "arbitrary"),
                                             vmem_limit_bytes=VMEM_LIMIT),
        name="proj",
    )(x, cos, sin, *weights)


def _attn_kernel(qm_ref, dq_ref, kvm_ref, dk_ref, dv_ref, lam_ref, olat_ref, odiff_ref,
                 qd_ref, m_ref, l_ref, acc_ref, *, tq, tk, q_pos0, n_keys, lam_init):
    i = pl.program_id(1)

    lane = lax.broadcasted_iota(jnp.int32, (tq, LANES), 1)
    seg = lax.shift_right_logical(lane, DIFF_D_SHIFT)
    for p in range(N_PAIR):
        qp = dq_ref[0, :, p * LANES:(p + 1) * LANES].astype(F32)
        for j in range(MAPS_PER_PAIR):
            blk = MAPS_PER_PAIR * p + j
            qd_ref[blk * tq:(blk + 1) * tq] = jnp.where(seg == j, qp, 0.0).astype(BF16)

    q_lo = q_pos0 + i * tq
    q_hi = q_lo + tq - 1
    vis_all = jnp.minimum((q_lo // CHUNK + 1) * CHUNK, n_keys)
    vis_any = jnp.minimum((q_hi // CHUNK + 1) * CHUNK, n_keys)
    n_full = vis_all // tk
    n_tot = (vis_any + tk - 1) // tk

    merge = max(1, MERGE_ROWS // tq)
    g_mla, g_diff = min(MLA_H, merge), min(MAPS_PER_PAIR, merge)
    work = [(b, g_mla) for b in range(0, MLA_H, g_mla)]
    work += [(MLA_H + b, g_diff) for b in range(0, MAPS_PER_PAIR * N_PAIR, g_diff)]

    def step(start, width, masked, first=False):
        if masked:
            k_pos = start + lax.broadcasted_iota(jnp.int32, (1, width), 1)
            k_chunk = jnp.where(k_pos < n_keys, lax.shift_right_logical(k_pos, CHUNK_SHIFT), jnp.int32(2 ** 30))
            visible = {}
            for n_rows in sorted({g * tq for _, g in work}):
                q_row = lax.broadcasted_iota(jnp.int32, (n_rows, 1), 0) & (tq - 1)
                visible[n_rows] = k_chunk <= lax.shift_right_logical(q_lo + q_row, CHUNK_SHIFT)
        for blk, g in work:
            n_rows = g * tq
            rows = slice(blk * tq, blk * tq + n_rows)
            if blk < MLA_H:
                q = qm_ref[0, blk] if g == 1 else qm_ref[0, blk:blk + g].reshape(n_rows, QK_PAD)
                k = kvm_ref[0, pl.ds(start, width), :]
                v = kvm_ref[0, pl.ds(start, width), 0:KV_LORA]
            else:
                p = (blk - MLA_H) // MAPS_PER_PAIR
                q = qd_ref[(blk - MLA_H) * tq:(blk - MLA_H) * tq + n_rows]
                k = dk_ref[0, pl.ds(start, width), p * LANES:(p + 1) * LANES]
                v = dv_ref[0, pl.ds(start, width), p * LANES:(p + 1) * LANES]
            s = _dot_nt(q, k)
            if masked:
                s = jnp.where(visible[n_rows], s, NEG_INF)
            s_max = jnp.max(s, axis=1, keepdims=True)
            if first:
                m_new = jnp.broadcast_to(s_max, (n_rows, LANES))
            else:
                m_old = m_ref[rows]
                m_new = jnp.maximum(m_old, s_max)
                alpha = jnp.exp2(m_old - m_new)
            ps = [jnp.exp2(s[:, c * LANES:(c + 1) * LANES] - m_new) for c in range(width // LANES)]
            pv = _dot(jnp.concatenate(ps, axis=1).astype(BF16), v)
            if first:
                l_ref[rows] = sum(ps)
                acc_ref[rows] = pv
            else:
                l_ref[rows] = alpha * l_ref[rows] + sum(ps)
                acc_ref[rows] = alpha * acc_ref[rows] + pv
            m_ref[rows] = m_new

    def full_steps(n, base):
        def body(j, c):
            for u in range(n):
                step(pl.multiple_of((base + n * j + u) * tk, tk), tk, False)
            return c
        return body

    def masked_step(kt, c):
        step(pl.multiple_of(kt * tk, tk), tk, True)
        return c

    @pl.when(n_full > 0)
    def _():
        step(0, tk, False, first=True)

    @pl.when(n_full == 0)
    def _():
        step(0, tk, True, first=True)

    n_rest = jnp.maximum(n_full - 1, 0)
    c4 = n_rest // 4
    c2 = (n_rest - 4 * c4) // 2
    c1 = n_rest - 4 * c4 - 2 * c2
    lax.fori_loop(0, c4, full_steps(4, 1), 0)
    lax.fori_loop(0, c2, full_steps(2, 1 + 4 * c4), 0)
    lax.fori_loop(0, c1, full_steps(1, 1 + 4 * c4 + 2 * c2), 0)
    lax.fori_loop(jnp.maximum(n_full, 1), n_tot, masked_step, 0)

    def normalised(blk):
        rows = slice(blk * tq, (blk + 1) * tq)
        return acc_ref[rows] / jnp.sum(l_ref[rows], axis=1, keepdims=True)

    for h in range(MLA_H):
        olat_ref[0, :, h * LANES:(h + 1) * LANES] = normalised(h).astype(BF16)

    lam = (jnp.exp(jnp.sum(lam_ref[0:1] * lam_ref[1:2], axis=1, keepdims=True))
           - jnp.exp(jnp.sum(lam_ref[2:3] * lam_ref[3:4], axis=1, keepdims=True)) + lam_init)
    for p in range(N_PAIR):
        n = [normalised(MLA_H + MAPS_PER_PAIR * p + j) for j in range(MAPS_PER_PAIR)]
        head0 = n[0] - lam * n[1]
        head1 = n[2] - lam * n[3]
        odiff_ref[0, :, p * LANES:(p + 1) * LANES] = jnp.where(lane < 2 * DIFF_D, head0, head1)


def _attention(qm, dqb, kvm, dkb, dvb, lam_rows, *, q_pos0, n_keys, lam_init):
    b, _, l, _ = qm.shape
    lk = kvm.shape[1]
    tq = min(ATTN_Q_ROWS, l)
    tk = ATTN_KEY_TILE
    assert l % tq == 0 and lk % tk == 0 and tq & (tq - 1) == 0
    n_rows = (MLA_H + MAPS_PER_PAIR * N_PAIR) * tq
    kern = functools.partial(_attn_kernel, tq=tq, tk=tk, q_pos0=q_pos0, n_keys=n_keys, lam_init=lam_init)
    per_batch = lambda width: pl.BlockSpec((1, lk, width), lambda j, i: (j, 0, 0))
    return pl.pallas_call(
        kern,
        grid=(b, l // tq),
        in_specs=[pl.BlockSpec((1, MLA_H, tq, QK_PAD), lambda j, i: (j, 0, i, 0)),
                  pl.BlockSpec((1, tq, DIFF_W), lambda j, i: (j, i, 0)),
                  per_batch(QK_PAD), per_batch(DIFF_W), per_batch(DIFF_W),
                  pl.BlockSpec(lam_rows.shape, lambda j, i: (0, 0))],
        out_specs=(pl.BlockSpec((1, tq, MLA_H * KV_LORA), lambda j, i: (j, i, 0)),
                   pl.BlockSpec((1, tq, DIFF_W), lambda j, i: (j, i, 0))),
        out_shape=(jax.ShapeDtypeStruct((b, l, MLA_H * KV_LORA), BF16),
                   jax.ShapeDtypeStruct((b, l, DIFF_W), F32)),
        scratch_shapes=[pltpu.VMEM((MAPS_PER_PAIR * N_PAIR * tq, LANES), BF16),
                        pltpu.VMEM((n_rows, LANES), F32),
                        pltpu.VMEM((n_rows, LANES), F32),
                        pltpu.VMEM((n_rows, LANES), F32)],
        compiler_params=pltpu.CompilerParams(dimension_semantics=("arbitrary", "arbitrary"),
                                             vmem_limit_bytes=VMEM_LIMIT),
        name="attn",
    )(qm, dqb, kvm, dkb, dvb, lam_rows)


def _post_kernel(x_ref, olat_ref, odiff_ref, cnt_in_ref, su_ref, gattn_ref, wg_ref, wuv_ref, womla_ref, bd_ref,
                 gsub_ref, wodiff_ref, wout_ref, gffn_ref, rwt_ref, rb_ref,
                 h_ref, tok_ref, topi_ref, rank_ref, gate_ref, cnt_ref, *, lam_init):
    tm = x_ref.shape[0]
    x = x_ref[...]
    xn = _rms(x, gattn_ref[...], RMS_EPS).astype(BF16)
    g = _dot(xn, wg_ref[...])
    y_mla = _dot(_dot(olat_ref[...], wuv_ref[...]).astype(BF16), womla_ref[...])

    od = odiff_ref[...]
    sq = od * od
    sq_hi = sq.astype(BF16)
    sq_lo = (sq - sq_hi.astype(F32)).astype(BF16)
    ms = _dot(sq_hi, bd_ref[...]) + _dot(sq_lo, bd_ref[...])
    on = od * lax.rsqrt(ms + SUBLN_EPS) * gsub_ref[...] * (1.0 - lam_init)
    y_diff = _dot(on.astype(BF16), wodiff_ref[...])

    merged = jax.nn.sigmoid(g[:, :D_MODEL]) * y_mla + jax.nn.sigmoid(g[:, D_MODEL:]) * y_diff
    h = x + _dot(merged.astype(BF16), wout_ref[...])
    h_ref[...] = h
    tok = _rms(h, gffn_ref[...], RMS_EPS).astype(BF16)
    tok_ref[...] = _pack_bf16_pairs(tok)

    logits = _dot_nt(rwt_ref[...], tok) + rb_ref[:, 0:1]
    e_iota = lax.broadcasted_iota(jnp.int32, (N_EXPERTS, tm), 0).astype(F32)
    work = logits
    vals, hots = [], []
    for k in range(TOP_K):
        mx = jnp.max(work, axis=0, keepdims=True)
        idx = jnp.min(jnp.where(work == mx, e_iota, float(N_EXPERTS)), axis=0, keepdims=True)
        hot = e_iota == idx
        work = jnp.where(hot, -jnp.inf, work)
        vals.append(mx)
        hots.append(hot)
        topi_ref[k:k + 1, :] = idx.astype(jnp.int32)
    exps = [jnp.exp(v - vals[0]) for v in vals]
    den = exps[0] + exps[1] + exps[2] + exps[3]
    for k in range(TOP_K):
        gate_ref[k:k + 1, :] = exps[k] / den

    @pl.when(pl.program_id(0) == 0)
    def _():
        cnt_ref[...] = cnt_in_ref[...]

    sel = sum(jnp.where(hot, 1.0, 0.0) for hot in hots)
    before = _dot(sel.astype(BF16), su_ref[...]) + cnt_ref[:, 0:1]
    for k in range(TOP_K):
        rank_ref[k:k + 1, :] = jnp.sum(jnp.where(hots[k], before, 0.0), axis=0, keepdims=True).astype(jnp.int32)
    cnt_ref[...] = cnt_ref[...] + jnp.sum(sel, axis=1, keepdims=True)


def _post(x2d, olat2d, odiff2d, cnt_in, w, *, lam_init):
    t, d = x2d.shape
    tm = min(POST_ROWS, t)
    assert t % tm == 0
    su =(lax.broadcasted_iota(jnp.int32, (tm, tm), 0) < lax.broadcasted_iota(jnp.int32, (tm, tm), 1)).astype(BF16)
    weights = (w['g_attn'], w['w_g'], w['w_uv_bd'], w['w_o_mla'], w['bd64'], w['g_subln'], w['w_o_diff'],
               w['w_out'], w['g_ffn'], w['rw_t'], w['rb'])

    def full(a):
        return pl.BlockSpec(a.shape, lambda i, n=a.ndim: (0,) * n)

    tokspec = lambda width: pl.BlockSpec((tm, width), lambda i: (i, 0))
    small = pl.BlockSpec((TOP_K, tm), lambda i: (0, i))
    kern = functools.partial(_post_kernel, lam_init=lam_init)
    return pl.pallas_call(
        kern,
        grid=(t // tm,),
        in_specs=[tokspec(d), tokspec(MLA_H * KV_LORA), tokspec(DIFF_W), full(cnt_in), full(su)]
                 + [full(a) for a in weights],
        out_specs=(tokspec(d), tokspec(d // 2), small, small, small, full(cnt_in)),
        out_shape=(jax.ShapeDtypeStruct((t, d), F32), jax.ShapeDtypeStruct((t, d // 2), jnp.uint32),
                   jax.ShapeDtypeStruct((TOP_K, t), jnp.int32), jax.ShapeDtypeStruct((TOP_K, t), jnp.int32),
                   jax.ShapeDtypeStruct((TOP_K, t), F32), jax.ShapeDtypeStruct(cnt_in.shape, F32)),
        compiler_params=pltpu.CompilerParams(dimension_semantics=("arbitrary",), vmem_limit_bytes=VMEM_LIMIT),
        name="post",
    )(x2d, olat2d, odiff2d, cnt_in, su, *weights)


def _dispatch_kernel(pos_ref, tok_ref, *rest):
    xs_ref, sems = rest[-2], rest[-1]
    tm = tok_ref.shape[0]

    for t in range(tm):
        for k in range(TOP_K):
            q = k % N_DMA_PRIORITIES
            pltpu.make_async_copy(tok_ref.at[pl.ds(t, 1)], xs_ref.at[pl.ds(pos_ref[k, t], 1)],
                                  sems.at[q]).start(priority=q)
    for k in range(TOP_K):
        pltpu.make_async_copy(tok_ref, xs_ref.at[pl.ds(0, tm)], sems.at[k % N_DMA_PRIORITIES]).wait()


def _dispatch(pos, tok, xs, n_slots):
    t, d = tok.shape
    tm = min(ROW_DMA_TILE, t)
    assert t % tm == 0
    in_specs = [pl.BlockSpec((TOP_K, tm), lambda i: (0, i), memory_space=pltpu.SMEM),
                pl.BlockSpec((tm, d), lambda i: (i, 0))]
    args = [pos, tok]
    aliases = {}
    if xs is not None:
        in_specs.append(pl.BlockSpec(memory_space=pl.ANY))
        args.append(xs)
        aliases = {2: 0}
    return pl.pallas_call(
        _dispatch_kernel,
        grid=(t // tm,),
        in_specs=in_specs,
        out_specs=pl.BlockSpec(memory_space=pl.ANY),
        out_shape=jax.ShapeDtypeStruct((n_slots, d), tok.dtype),
        scratch_shapes=[pltpu.SemaphoreType.DMA((N_DMA_PRIORITIES,))],
        input_output_aliases=aliases,
        compiler_params=pltpu.CompilerParams(dimension_semantics=("arbitrary",), has_side_effects=True),
        name="dispatch",
    )(*args)


def _expert_kernel(be_ref, nv_ref, xs_ref, wgu_ref, bgu_ref, wd_ref, bd_ref, ys_ref, wgu_bf_ref, wd_bf_ref):
    i = pl.program_id(0)
    r = xs_ref.shape[0]
    nv = nv_ref[i]

    @pl.when(nv > 0)
    def _():
        @pl.when(jnp.logical_or(i == 0, be_ref[i] != be_ref[jnp.maximum(i - 1, 0)]))
        def _():
            wgu_bf_ref[...] = wgu_ref[0].astype(BF16)
            wd_bf_ref[...] = wd_ref[0].astype(BF16)

        rows = lax.broadcasted_iota(jnp.int32, (r, 1), 0)
        x = _unpack_bf16_pairs(jnp.where(rows < nv, xs_ref[...], jnp.uint32(0)))
        h = _dot(x, wgu_bf_ref[...]) + bgu_ref[0]
        gate = jnp.minimum(h[:, :D_FF], SWIGLU_LIMIT)
        up = jnp.clip(h[:, D_FF:], -SWIGLU_LIMIT, SWIGLU_LIMIT)
        act = gate * jax.nn.sigmoid(SWIGLU_ALPHA * gate) * (up + 1.0)
        ys_ref[...] = _dot(act.astype(BF16), wd_bf_ref[...]) + bd_ref[0]


def _experts(block_e, n_valid, xs, wgu, bgu, wd, bd, rows_per_block):
    n_slots = xs.shape[0]
    d = D_MODEL
    r = rows_per_block
    n_blocks = n_slots // r
    grid_spec = pltpu.PrefetchScalarGridSpec(
        num_scalar_prefetch=2,
        grid=(n_blocks,),
        in_specs=[pl.BlockSpec((r, xs.shape[1]), lambda i, be, nv: (i, 0)),
                  pl.BlockSpec((1, d, 2 * D_FF), lambda i, be, nv: (be[i], 0, 0)),
                  pl.BlockSpec((1, 1, 2 * D_FF), lambda i, be, nv: (be[i], 0, 0)),
                  pl.BlockSpec((1, D_FF, d), lambda i, be, nv: (be[i], 0, 0)),
                  pl.BlockSpec((1, 1, d), lambda i, be, nv: (be[i], 0, 0))],
        out_specs=pl.BlockSpec((r, d), lambda i, be, nv: (i, 0)),
        scratch_shapes=[pltpu.VMEM((d, 2 * D_FF), BF16), pltpu.VMEM((D_FF, d), BF16)],
    )
    return pl.pallas_call(
        _expert_kernel,
        grid_spec=grid_spec,
        out_shape=jax.ShapeDtypeStruct((n_slots, d), F32),
        compiler_params=pltpu.CompilerParams(dimension_semantics=("arbitrary",), vmem_limit_bytes=VMEM_LIMIT),
        name="experts",
    )(block_e, n_valid, xs, wgu, bgu, wd, bd)


def _combine_kernel(pos_ref, pos_next_ref, gate_ref, h_ref, gfin_ref, ys_ref, out_ref, buf0_ref, buf1_ref, sems):
    tm = h_ref.shape[0]
    i = pl.program_id(0)
    last = pl.num_programs(0) - 1
    slot = i % 2
    bufs = (buf0_ref, buf1_ref)

    def gather(idx_ref, s):
        for t in range(tm):
            for k in range(TOP_K):
                q = k % N_DMA_PRIORITIES
                pltpu.make_async_copy(ys_ref.at[pl.ds(idx_ref[k, t], 1)], bufs[s].at[k, pl.ds(t, 1)],
                                      sems.at[s, q]).start(priority=q)

    def wait_tile(s):
        for k in range(TOP_K):
            pltpu.make_async_copy(ys_ref.at[pl.ds(0, tm)], bufs[s].at[k], sems.at[s, k % N_DMA_PRIORITIES]).wait()

    @pl.when(i == 0)
    def _():
        gather(pos_ref, 0)

    def tile(s):
        wait_tile(s)
        gather(pos_next_ref, 1 - s)
        eye = lax.broadcasted_iota(jnp.int32, (tm, tm), 0) == lax.broadcasted_iota(jnp.int32, (tm, tm), 1)
        acc = h_ref[...]
        for k in range(TOP_K):
            g_col = jnp.sum(jnp.where(eye, gate_ref[k:k + 1, :], 0.0), axis=1, keepdims=True)
            acc = acc + g_col * bufs[s][k]
        out_ref[...] = _rms(acc, gfin_ref[...], RMS_EPS)

        @pl.when(i == last)
        def _():
            wait_tile(1 - s)

    for s in range(2):
        pl.when(slot == s)(functools.partial(tile, s))


def _combine(pos, gate, h, g_final, ys):
    t, d = h.shape
    tm = min(ROW_DMA_TILE, t)
    assert t % tm == 0
    n_tiles = t // tm
    return pl.pallas_call(
        _combine_kernel,
        grid=(n_tiles,),
        in_specs=[pl.BlockSpec((TOP_K, tm), lambda i: (0, i), memory_space=pltpu.SMEM),
                  pl.BlockSpec((TOP_K, tm), lambda i: (0, jnp.minimum(i + 1, n_tiles - 1)), memory_space=pltpu.SMEM),
                  pl.BlockSpec((TOP_K, tm), lambda i: (0, i)),
                  pl.BlockSpec((tm, d), lambda i: (i, 0)),
                  pl.BlockSpec(g_final.shape, lambda i: (0, 0)),
                  pl.BlockSpec(memory_space=pl.ANY)],
        out_specs=pl.BlockSpec((tm, d), lambda i: (i, 0)),
        out_shape=jax.ShapeDtypeStruct((t, d), F32),
        scratch_shapes=[pltpu.VMEM((TOP_K, tm, d), F32), pltpu.VMEM((TOP_K, tm, d), F32),
                        pltpu.SemaphoreType.DMA((2, N_DMA_PRIORITIES))],
        compiler_params=pltpu.CompilerParams(dimension_semantics=("arbitrary",), vmem_limit_bytes=VMEM_LIMIT),
        name="combine",
    )(pos, pos, gate, h, g_final, ys)


def _rope_tables(pos):
    half = DIFF_D // 2
    inv = ROPE_THETA ** (-jnp.arange(half, dtype=F32) / half)
    ang = pos.astype(F32)[:, None] * inv[None, :]
    cos, sin = jnp.cos(ang), jnp.sin(ang)
    reps = LANES // DIFF_D
    return (jnp.tile(jnp.concatenate([cos, cos], axis=1), (1, reps)),
            jnp.tile(jnp.concatenate([-sin, sin], axis=1), (1, reps)))


def _prep_weights(w_in, g_attn_norm, g_q_a, w_uq, g_kv_a, w_uk, w_uv, w_o_mla, g_subln, w_o_diff, w_out,
                  g_ffn_norm, router_w, router_b):
    o_q = 0
    o_kv = o_q + Q_LORA
    o_kr = o_kv + KV_LORA
    o_dq = o_kr + MLA_ROPE
    o_dk = o_dq + DIFF_W
    o_dv = o_dk + DIFF_W
    o_g = o_dv + DIFF_W
    w_a = jnp.concatenate([w_in[:, o_q:o_kv], w_in[:, o_kv:o_kr], w_in[:, o_dq:o_dk], w_in[:, o_dk:o_dv],
                           w_in[:, o_dv:o_g], w_in[:, o_kr:o_dq],
                           jnp.zeros((D_MODEL, LANES - MLA_ROPE), w_in.dtype)], axis=1).astype(BF16)
    w_g = w_in[:, o_g:].astype(BF16)
    uq = w_uq.reshape(Q_LORA, MLA_H, MLA_QK)
    w_uq_p = jnp.concatenate([uq[:, :, :MLA_NOPE].reshape(Q_LORA, MLA_H * MLA_NOPE),
                              uq[:, :, MLA_NOPE:].reshape(Q_LORA, MLA_H * MLA_ROPE)], axis=1).astype(BF16)
    same_head = jnp.eye(MLA_H, dtype=bool)[:, None, :, None]
    ukt = jnp.transpose(w_uk, (1, 2, 0))
    w_n = jnp.where(same_head, ukt[:, :, None, :], 0.0).reshape(MLA_H * MLA_NOPE, MLA_H * KV_LORA).astype(BF16)
    place = jnp.eye(MLA_ROPE, KV_LORA, dtype=F32)
    w_e = jnp.where(same_head, place[None, :, None, :], 0.0).reshape(MLA_H * MLA_ROPE, MLA_H * KV_LORA).astype(BF16)
    uv = jnp.transpose(w_uv, (1, 0, 2))
    w_uv_bd = jnp.where(same_head, uv[:, :, None, :], 0.0).reshape(MLA_H * KV_LORA, MLA_H * MLA_V).astype(BF16)
    grp = jnp.arange(DIFF_W) // (2 * DIFF_D)
    bd64 = jnp.where(grp[:, None] == grp[None, :], 1.0 / (2 * DIFF_D), 0.0).astype(BF16)
    return {
        'g_attn': g_attn_norm.reshape(1, D_MODEL), 'w_a': w_a, 'w_g': w_g,
        'g_q_a': g_q_a.reshape(1, Q_LORA), 'w_uq': w_uq_p, 'w_n': w_n, 'w_e': w_e,
        'g_kv_a': g_kv_a.reshape(1, KV_LORA), 'w_uv_bd': w_uv_bd, 'w_o_mla': w_o_mla.astype(BF16),
        'bd64': bd64, 'g_subln': jnp.tile(g_subln, DIFF_H).reshape(1, DIFF_W),
        'w_o_diff': w_o_diff.astype(BF16), 'w_out': w_out.astype(BF16),
        'g_ffn': g_ffn_norm.reshape(1, D_MODEL), 'rw_t': router_w.T.astype(BF16),
        'rb': jnp.broadcast_to(router_b.reshape(N_EXPERTS, 1), (N_EXPERTS, LANES)),
    }


def _pad_keys(a, tk):
    lk = a.shape[1]
    pad = (-lk) % tk
    if pad == 0:
        return a
    return jnp.concatenate([a, jnp.zeros((a.shape[0], pad) + a.shape[2:], a.dtype)], axis=1)


def _slot_base(topi, pstart):
    e = jnp.arange(N_EXPERTS, dtype=jnp.int32)
    return jnp.sum(jnp.where(topi[..., None] == e, pstart, 0), axis=-1)


def kernel(x_prompt, x_sample, cache_mla_ckv, cache_mla_kpe, cache_diff_k, cache_diff_v, g_attn_norm, w_in, g_q_a, w_uq, g_kv_a, w_uk, w_uv, w_o_mla, lambda_q1, lambda_k1, lambda_q2, lambda_k2, g_subln, w_o_diff, w_out, g_ffn_norm, router_w, router_b, w_gate_up, b_gate_up, w_down, b_down, g_final):
    depth = w_in.shape[0]
    assert depth == 1
    li = 0
    lam_init = 0.8 - 0.6 * math.exp(-0.3 * li)
    bp, lp, d = x_prompt.shape
    bs, ls, _ = x_sample.shape
    past = cache_mla_ckv.shape[2]
    tk = ATTN_KEY_TILE

    w = _prep_weights(w_in[li], g_attn_norm[li], g_q_a[li], w_uq[li], g_kv_a[li], w_uk[li], w_uv[li], w_o_mla[li],
                      g_subln[li], w_o_diff[li], w_out[li], g_ffn_norm[li], router_w[li], router_b[li])
    lam_rows = jnp.stack([lambda_q1[li], lambda_k1[li], lambda_q2[li], lambda_k2[li]]).astype(F32)

    cos_p, sin_p = _rope_tables(jnp.arange(lp, dtype=jnp.int32))
    ckv_p, kpe_p, dk_p, dv_p, qm_p, kvm_p, dqb_p, dkb_p, dvb_p = _proj(x_prompt, cos_p, sin_p, w)
    olat_p, odiff_p = _attention(qm_p, dqb_p, kvm_p, dkb_p, dvb_p, lam_rows, q_pos0=0, n_keys=lp, lam_init=lam_init)

    cos_s, sin_s = _rope_tables(jnp.arange(past, past + ls, dtype=jnp.int32))
    ckv_s, kpe_s, dk_s, dv_s, qm_s, kvm_s, dqb_s, dkb_s, dvb_s = _proj(x_sample, cos_s, sin_s, w)
    kvm_past = jnp.concatenate([cache_mla_ckv[li], cache_mla_kpe[li],
                                jnp.zeros((bs, past, QK_PAD - KV_LORA - MLA_ROPE), F32)], axis=2).astype(BF16)
    kvm_all = _pad_keys(jnp.concatenate([kvm_past, kvm_s], axis=1), tk)
    dkb_all = _pad_keys(jnp.concatenate([cache_diff_k[li].reshape(bs, past, DIFF_W).astype(BF16), dkb_s], axis=1), tk)
    dvb_all = _pad_keys(jnp.concatenate([cache_diff_v[li].reshape(bs, past, DIFF_W).astype(BF16), dvb_s], axis=1), tk)
    olat_s, odiff_s = _attention(qm_s, dqb_s, kvm_all, dkb_all, dvb_all, lam_rows,
                                 q_pos0=past, n_keys=past + ls, lam_init=lam_init)

    tp, ts = bp * lp, bs * ls
    cnt0 = jnp.zeros((N_EXPERTS, LANES), F32)
    h_p, tok_p, topi_p, rank_p, gate_p, cnt1 = _post(x_prompt.reshape(tp, d), olat_p.reshape(tp, -1),
                                                     odiff_p.reshape(tp, -1), cnt0, w, lam_init=lam_init)
    h_s, tok_s, topi_s, rank_s, gate_s, cnt2 = _post(x_sample.reshape(ts, d), olat_s.reshape(ts, -1),
                                                     odiff_s.reshape(ts, -1), cnt1, w, lam_init=lam_init)

    r = EXPERT_BLOCK_ROWS
    n_assign = (tp + ts) * TOP_K
    n_blocks = -(-n_assign // r) + N_EXPERTS
    n_slots = n_blocks * r
    counts = cnt2[:, 0].astype(jnp.int32)
    padded = (counts + r - 1) // r * r
    pend = jnp.cumsum(padded)
    pstart = pend - padded
    blk_start = jnp.arange(n_blocks, dtype=jnp.int32) * r
    block_e = jnp.minimum(jnp.sum((pend[None, :] <= blk_start[:, None]).astype(jnp.int32), axis=1), N_EXPERTS - 1)
    seg_end = _slot_base(block_e, pstart + counts)
    n_valid = jnp.where(blk_start < pend[-1], jnp.clip(seg_end - blk_start, 0, r), 0).astype(jnp.int32)
    pos_p = rank_p + _slot_base(topi_p, pstart)
    pos_s = rank_s + _slot_base(topi_s, pstart)

    xs = _dispatch(pos_p, tok_p, None, n_slots)
    xs = _dispatch(pos_s, tok_s, xs, n_slots)
    ys = _experts(block_e, n_valid, xs, w_gate_up[li], b_gate_up[li].reshape(N_EXPERTS, 1, -1),
                  w_down[li], b_down[li].reshape(N_EXPERTS, 1, -1), r)
    gfin = g_final.reshape(1, d)
    y_p = _combine(pos_p, gate_p, h_p, gfin, ys)
    y_s = _combine(pos_s, gate_s, h_s, gfin, ys)

    def stack(a, shape):
        return a.reshape((1,) + shape)

    return (y_p.reshape(bp, lp, d), y_s.reshape(bs, ls, d),
            stack(ckv_p, (bp, lp, KV_LORA)), stack(kpe_p, (bp, lp, MLA_ROPE)),
            stack(dk_p, (bp, lp, DIFF_H, 2, DIFF_D)), stack(dv_p, (bp, lp, DIFF_H, 2 * DIFF_D)),
            stack(ckv_s, (bs, ls, KV_LORA)), stack(kpe_s, (bs, ls, MLA_ROPE)),
            stack(dk_s, (bs, ls, DIFF_H, 2, DIFF_D)), stack(dv_s, (bs, ls, DIFF_H, 2 * DIFF_D)))
```
